```python
import math
import jax, jax.numpy as jnp
from jax import lax
import numpy as np

D_MODEL = 1024
BATCH = 4
SEQ = 4096
DEPTH = 4

PLE_DIM = 256
HEAD_DIM = 64
ROPE_DIM = HEAD_DIM // 4
ROPE_THETA = 500000.0
NORM_EPS = 1e-6
NEG_INF = -1e30

MOBA_HEADS = 8
MOBA_BLOCK = 256
MOBA_TOPK = 3
MOBA_QCHUNK = 64
MOBA_WIDTH = MOBA_HEADS * HEAD_DIM

DIL_WINDOWS = (128, 512, 2048)
DIL_RATES = (1, 4, 16)
DIL_GROUPS = 3
DIL_HEADS = 8
DIL_NKEYS = DIL_WINDOWS[0] // DIL_RATES[0] + 1
DIL_QBLOCK = 64
DIL_WIDTH = DIL_HEADS * HEAD_DIM

SSM_INNER = D_MODEL
SSM_HEAD_DIM = 64
SSM_HEADS = SSM_INNER // SSM_HEAD_DIM
SSM_GROUPS = 4
SSM_STATE = 128
SSM_CONV = 4
SSM_CHUNK = 128
SSM_XBC = SSM_INNER + 2 * SSM_GROUPS * SSM_STATE

FFN_DIM = 2816
FFN_CONV = 3

N_BRANCH = 3
IN_SIZES = (3 * MOBA_WIDTH, 3 * DIL_GROUPS * DIL_WIDTH, SSM_INNER, SSM_XBC, SSM_HEADS, N_BRANCH * D_MODEL)
IN_COLS = sum(IN_SIZES)

kernel_name = "hybrid_moba_ssd_dilated_block"


def rms_norm(x, g):
    xf = x.astype(jnp.float32)
    y = xf * lax.rsqrt(jnp.mean(xf * xf, axis=-1, keepdims=True) + NORM_EPS)
    return (y * g.astype(jnp.float32)).astype(x.dtype)


def rope_tables(positions):
    inv = ROPE_THETA ** (-jnp.arange(0, ROPE_DIM, 2, dtype=jnp.float32) / ROPE_DIM)
    ang = positions.astype(jnp.float32)[..., None] * inv
    return jnp.cos(ang), jnp.sin(ang)


def apply_rope(x, cos, sin):
    half = ROPE_DIM // 2
    bshape = cos.shape[:2] + (1,) * (x.ndim - 3) + (half,)
    c = cos.reshape(bshape).astype(x.dtype)
    s = sin.reshape(bshape).astype(x.dtype)
    x1 = x[..., :half]
    x2 = x[..., half:ROPE_DIM]
    return jnp.concatenate([x1 * c - x2 * s, x2 * c + x1 * s, x[..., ROPE_DIM:]], axis=-1)


def causal_dwconv(x, w, b):
    k_width, chans = w.shape
    y = lax.conv_general_dilated(
        x, w.astype(x.dtype)[:, None, :], window_strides=(1,), padding=((k_width - 1, 0),),
        dimension_numbers=('NWC', 'WIO', 'NWC'), feature_group_count=chans)
    return y + b.astype(x.dtype)


def moba_attention(q, k, v):
    bsz, S, H, dh = q.shape
    scale = dh ** -0.5
    qt, kt, vt = (t.transpose(0, 2, 1, 3) for t in (q, k, v))
    nb = -(-S // MOBA_BLOCK)
    pad = nb * MOBA_BLOCK - S
    kp = jnp.pad(kt, ((0, 0), (0, 0), (0, pad), (0, 0)))
    vp = jnp.pad(vt, ((0, 0), (0, 0), (0, pad), (0, 0)))
    kb = kp.reshape(bsz, H, nb, MOBA_BLOCK, dh)
    vb = vp.reshape(bsz, H, nb, MOBA_BLOCK, dh)
    kmean = kb.astype(jnp.float32).mean(axis=3)
    topk = min(MOBA_TOPK, nb)
    blk_ids = jnp.arange(nb)
    gather = jax.vmap(jax.vmap(lambda t, i: t[i]))

    def one_chunk(ci):
        t0 = ci * MOBA_QCHUNK
        jb = t0 // MOBA_BLOCK
        qpos = t0 + jnp.arange(MOBA_QCHUNK)
        qc = lax.dynamic_slice_in_dim(qt, t0, MOBA_QCHUNK, axis=2)
        score = jnp.einsum('bhqd,bhnd->bhqn', qc.astype(jnp.float32), kmean)
        score = jnp.where(blk_ids < jb, score, -jnp.inf)
        _, sel = lax.top_k(score, topk)
        sel_ok = sel < jb
        ks = gather(kb, sel)
        vs = gather(vb, sel)
        l_sel = jnp.einsum('bhqd,bhqtkd->bhqtk', qc, ks).astype(jnp.float32) * scale
        l_sel = jnp.where(sel_ok[..., None], l_sel, NEG_INF)
        l_sel = l_sel.reshape(bsz, H, MOBA_QCHUNK, topk * MOBA_BLOCK)
        k_own = lax.dynamic_slice_in_dim(kp, jb * MOBA_BLOCK, MOBA_BLOCK, axis=2)
        v_own = lax.dynamic_slice_in_dim(vp, jb * MOBA_BLOCK, MOBA_BLOCK, axis=2)
        l_own = jnp.einsum('bhqd,bhkd->bhqk', qc, k_own).astype(jnp.float32) * scale
        kpos = jb * MOBA_BLOCK + jnp.arange(MOBA_BLOCK)
        l_own = jnp.where(kpos[None, :] <= qpos[:, None], l_own, NEG_INF)
        w = jax.nn.softmax(jnp.concatenate([l_own, l_sel], axis=-1), axis=-1).astype(v.dtype)
        w_own = w[..., :MOBA_BLOCK]
        w_sel = w[..., MOBA_BLOCK:].reshape(bsz, H, MOBA_QCHUNK, topk, MOBA_BLOCK)
        return (jnp.einsum('bhqk,bhkd->bhqd', w_own, v_own)
                + jnp.einsum('bhqtk,bhqtkd->bhqd', w_sel, vs))

    out = lax.map(one_chunk, jnp.arange(S // MOBA_QCHUNK))
    out = jnp.moveaxis(out, 0, 2).reshape(bsz, H, S, dh)
    return out.transpose(0, 2, 1, 3).reshape(bsz, S, H * dh)


def dilated_attention(q, k, v):
    bsz, S, G, hd, dh = q.shape
    scale = dh ** -0.5
    qt, kt, vt = (t.transpose(0, 2, 3, 1, 4) for t in (q, k, v))
    rates = jnp.array(DIL_RATES, dtype=jnp.int32)
    offs = jnp.arange(DIL_NKEYS, dtype=jnp.int32)
    gather = jax.vmap(lambda t, i: jnp.take(t, i, axis=2), in_axes=(1, 0), out_axes=1)

    def one_block(bi):
        t0 = bi * DIL_QBLOCK
        qpos = t0 + jnp.arange(DIL_QBLOCK)
        idx = qpos[None, :, None] - rates[:, None, None] * offs[None, None, :]
        ok = idx >= 0
        idx = jnp.maximum(idx, 0)
        qb = lax.dynamic_slice_in_dim(qt, t0, DIL_QBLOCK, axis=3)
        kg = gather(kt, idx)
        vg = gather(vt, idx)
        l = jnp.einsum('bghqd,bghqkd->bghqk', qb, kg).astype(jnp.float32) * scale
        l = jnp.where(ok[None, :, None], l, NEG_INF)
        m = l.max(axis=-1, keepdims=True)
        e = jnp.exp(l - m)
        den = e.sum(axis=-1, keepdims=True)
        o_g = jnp.einsum('bghqk,bghqkd->bghqd', (e / den).astype(v.dtype), vg)
        lse = (m + jnp.log(den))[..., 0]
        alpha = jax.nn.softmax(lse, axis=1).astype(v.dtype)
        return jnp.einsum('bghq,bghqd->bhqd', alpha, o_g)

    out = lax.map(one_block, jnp.arange(S // DIL_QBLOCK))
    out = jnp.moveaxis(out, 0, 2).reshape(bsz, hd, S, dh)
    return out.transpose(0, 2, 1, 3).reshape(bsz, S, hd * dh)


def ssd_scan(x, dt, a, bm, cm):
    bsz, S, H, P = x.shape
    G, N = bm.shape[2], bm.shape[3]
    hg = H // G
    L = SSM_CHUNK
    nc = S // L
    xdt = (x.astype(jnp.float32) * dt[..., None]).reshape(bsz, nc, L, G, hg, P)
    adt = (dt * a).reshape(bsz, nc, L, G, hg).transpose(0, 3, 4, 1, 2)
    acs = jnp.cumsum(adt, axis=-1)
    bc = bm.astype(jnp.float32).reshape(bsz, nc, L, G, N)
    cc = cm.astype(jnp.float32).reshape(bsz, nc, L, G, N)
    causal = jnp.tril(jnp.ones((L, L), dtype=bool))
    decay = jnp.exp(jnp.where(causal, acs[..., :, None] - acs[..., None, :], -jnp.inf))
    cb = jnp.einsum('bclgn,bcsgn->bgcls', cc, bc)
    y_diag = jnp.einsum('bgcls,bghcls,bcsghp->bclghp', cb, decay, xdt)
    decay_states = jnp.exp(acs[..., -1:] - acs)
    states = jnp.einsum('bclgn,bghcl,bclghp->bcghpn', bc, decay_states, xdt)
    chunk_decay = jnp.exp(acs[..., -1])

    def step(h, inp):
        st, dc = inp
        return dc[..., None, None] * h + st, h

    h0 = jnp.zeros((bsz, G, hg, P, N), jnp.float32)
    _, prev = lax.scan(step, h0, (jnp.moveaxis(states, 1, 0), jnp.moveaxis(chunk_decay, -1, 0)))
    prev = jnp.moveaxis(prev, 0, 1)
    y_off = jnp.einsum('bclgn,bcghpn,bghcl->bclghp', cc, prev, jnp.exp(acs))
    return (y_diag + y_off).reshape(bsz, S, H, P)


def mamba2_mixer(z, xbc, dt_raw, conv_w, conv_b, dt_bias, a_log, d_skip, out_norm):
    bsz, S, _ = z.shape
    xbc = jax.nn.silu(causal_dwconv(xbc, conv_w, conv_b))
    xs, bm, cm = jnp.split(xbc, [SSM_INNER, SSM_INNER + SSM_GROUPS * SSM_STATE], axis=-1)
    dt = jax.nn.softplus(dt_raw.astype(jnp.float32) + dt_bias.astype(jnp.float32))
    a = -jnp.exp(a_log.astype(jnp.float32))
    xh = xs.reshape(bsz, S, SSM_HEADS, SSM_HEAD_DIM)
    y = ssd_scan(xh, dt, a,
                 bm.reshape(bsz, S, SSM_GROUPS, SSM_STATE), cm.reshape(bsz, S, SSM_GROUPS, SSM_STATE))
    y = y + xh.astype(jnp.float32) * d_skip.astype(jnp.float32)[None, None, :, None]
    y = y.reshape(bsz, S, SSM_INNER).astype(z.dtype)
    return rms_norm(y * jax.nn.silu(z), out_norm)


def setup_inputs(seed: int = 0) -> dict:
    key = jax.random.key(seed)
    ks = jax.random.split(key, 32)
    f32 = jnp.float32

    def nrm(k, shape, scale):
        return jax.random.normal(k, shape, f32) * scale

    res_scale = (2 * DEPTH) ** -0.5
    x = nrm(ks[0], (BATCH, SEQ, D_MODEL), 1.0)
    p = nrm(ks[1], (DEPTH, BATCH, SEQ, PLE_DIM), 1.0)
    offset = jax.random.randint(ks[2], (BATCH, 1), 0, 1024, dtype=jnp.int32)
    positions = (offset + jnp.arange(SEQ, dtype=jnp.int32)[None, :]).astype(jnp.int32)
    dt0 = jnp.exp(jax.random.uniform(ks[12], (DEPTH, SSM_HEADS), f32)
                  * (math.log(0.1) - math.log(0.001)) + math.log(0.001))
    return {
        "x": x,
        "p": p,
        "positions": positions,
        "norm_mix": 1.0 + nrm(ks[3], (DEPTH, D_MODEL), 0.1),
        "w_in": nrm(ks[4], (DEPTH, D_MODEL, IN_COLS), D_MODEL ** -0.5),
        "b_gate": nrm(ks[5], (DEPTH, N_BRANCH * D_MODEL), 0.1),
        "moba_q_norm": 1.0 + nrm(ks[6], (DEPTH, HEAD_DIM), 0.1),
        "moba_k_norm": 1.0 + nrm(ks[7], (DEPTH, HEAD_DIM), 0.1),
        "dil_q_norm": 1.0 + nrm(ks[8], (DEPTH, HEAD_DIM), 0.1),
        "dil_k_norm": 1.0 + nrm(ks[9], (DEPTH, HEAD_DIM), 0.1),
        "ssm_conv_w": nrm(ks[10], (DEPTH, SSM_CONV, SSM_XBC), SSM_CONV ** -0.5),
        "ssm_conv_b": nrm(ks[11], (DEPTH, SSM_XBC), 0.1),
        "ssm_dt_bias": dt0 + jnp.log(-jnp.expm1(-dt0)),
        "ssm_a_log": jnp.log(jax.random.uniform(ks[13], (DEPTH, SSM_HEADS), f32, 1.0, 16.0)),
        "ssm_d": 1.0 + nrm(ks[14], (DEPTH, SSM_HEADS), 0.1),
        "ssm_out_norm": 1.0 + nrm(ks[15], (DEPTH, SSM_INNER), 0.1),
        "w_br_moba": nrm(ks[16], (DEPTH, MOBA_WIDTH, D_MODEL), MOBA_WIDTH ** -0.5),
        "w_br_ssm": nrm(ks[17], (DEPTH, SSM_INNER, D_MODEL), SSM_INNER ** -0.5),
        "w_br_dil": nrm(ks[18], (DEPTH, DIL_WIDTH, D_MODEL), DIL_WIDTH ** -0.5),
        "w_out": nrm(ks[19], (DEPTH, D_MODEL, D_MODEL), D_MODEL ** -0.5 * res_scale),
        "norm_ffn": 1.0 + nrm(ks[20], (DEPTH, D_MODEL), 0.1),
        "w_up": nrm(ks[21], (DEPTH, D_MODEL, 2 * FFN_DIM), D_MODEL ** -0.5),
        "ffn_conv_w": nrm(ks[22], (DEPTH, FFN_CONV, 2 * FFN_DIM), FFN_CONV ** -0.5),
        "ffn_conv_b": nrm(ks[23], (DEPTH, 2 * FFN_DIM), 0.1),
        "w_down": nrm(ks[24], (DEPTH, FFN_DIM, D_MODEL), FFN_DIM ** -0.5 * res_scale),
        "norm_ple": 1.0 + nrm(ks[25], (DEPTH, D_MODEL), 0.1),
        "w_ple_gate": nrm(ks[26], (DEPTH, D_MODEL, D_MODEL), D_MODEL ** -0.5),
        "w_ple": nrm(ks[27], (DEPTH, PLE_DIM, D_MODEL), PLE_DIM ** -0.5 * res_scale),
    }


def reference(x, p, positions, norm_mix, w_in, b_gate, moba_q_norm, moba_k_norm, dil_q_norm,
              dil_k_norm, ssm_conv_w, ssm_conv_b, ssm_dt_bias, ssm_a_log, ssm_d, ssm_out_norm,
              w_br_moba, w_br_ssm, w_br_dil, w_out, norm_ffn, w_up, ffn_conv_w, ffn_conv_b,
              w_down, norm_ple, w_ple_gate, w_ple):
    bsz, S, _ = x.shape
    cos, sin = rope_tables(positions)
    split_pts = [int(v) for v in np.cumsum(IN_SIZES)[:-1]]
    for i in range(DEPTH):
        u = rms_norm(x, norm_mix[i])
        proj = u @ w_in[i]
        moba_qkv, dil_qkv, ssm_z, ssm_xbc, ssm_dt, gate_logits = jnp.split(proj, split_pts, axis=-1)

        mq, mk, mv = jnp.split(moba_qkv.reshape(bsz, S, 3, MOBA_HEADS, HEAD_DIM), 3, axis=2)
        mq = apply_rope(rms_norm(mq[:, :, 0], moba_q_norm[i]), cos, sin)
        mk = apply_rope(rms_norm(mk[:, :, 0], moba_k_norm[i]), cos, sin)
        out_a = moba_attention(mq, mk, mv[:, :, 0])

        out_b = mamba2_mixer(ssm_z, ssm_xbc, ssm_dt, ssm_conv_w[i], ssm_conv_b[i], ssm_dt_bias[i],
                             ssm_a_log[i], ssm_d[i], ssm_out_norm[i])

        dq, dk, dv = jnp.split(dil_qkv.reshape(bsz, S, 3, DIL_GROUPS, DIL_HEADS, HEAD_DIM), 3, axis=2)
        dq = apply_rope(rms_norm(dq[:, :, 0], dil_q_norm[i]), cos, sin)
        dk = apply_rope(rms_norm(dk[:, :, 0], dil_k_norm[i]), cos, sin)
        out_c = dilated_attention(dq, dk, dv[:, :, 0])

        gates = jax.nn.sigmoid(gate_logits + b_gate[i]).reshape(bsz, S, N_BRANCH, D_MODEL)
        merged = (gates[:, :, 0] * (out_a @ w_br_moba[i])
                  + gates[:, :, 1] * (out_b @ w_br_ssm[i])
                  + gates[:, :, 2] * (out_c @ w_br_dil[i]))
        x = x + merged @ w_out[i]

        up = causal_dwconv(rms_norm(x, norm_ffn[i]) @ w_up[i], ffn_conv_w[i], ffn_conv_b[i])
        ga, gb = jnp.split(up, 2, axis=-1)
        x = x + (jax.nn.silu(ga) * gb) @ w_down[i]

        pg = jax.nn.sigmoid(rms_norm(x, norm_ple[i]) @ w_ple_gate[i])
        x = x + (p[i] @ w_ple[i]) * pg
    return x
```

```python
import functools
import math

import numpy as np
import jax
import jax.numpy as jnp
from jax import lax
from jax.experimental import pallas as pl
from jax.experimental.pallas import tpu as pltpu

F32 = jnp.float32
BF16 = jnp.bfloat16

D_MODEL = 1024
PLE_DIM = 256
HEAD_DIM = 64
ROPE_DIM = HEAD_DIM // 4
ROPE_THETA = 500000.0
NORM_EPS = 1e-6
NEG_INF = -1e30

MOBA_HEADS = 8
MOBA_BLOCK = 256
MOBA_TOPK = 3
MOBA_WIDTH = MOBA_HEADS * HEAD_DIM

DIL_RATES = (1, 4, 16)
DIL_GROUPS = 3
DIL_HEADS = 8
DIL_WINDOW = 128
DIL_WIDTH = DIL_HEADS * HEAD_DIM
DIL_QBLOCK = 128

SSM_INNER = D_MODEL
SSM_HEAD_DIM = 64
SSM_HEADS = SSM_INNER // SSM_HEAD_DIM
SSM_GROUPS = 4
SSM_STATE = 128
SSM_CONV = 4
SSM_CHUNK = 128
SSM_XBC = SSM_INNER + 2 * SSM_GROUPS * SSM_STATE

FFN_DIM = 2816
FFN_CONV = 3
FFN_CHUNK = 256
N_BRANCH = 3

COL_MOBA = 0
COL_DIL = 3 * MOBA_WIDTH
COL_Z = COL_DIL + 3 * DIL_GROUPS * DIL_WIDTH
COL_XBC = COL_Z + SSM_INNER
COL_DT = COL_XBC + SSM_XBC
COL_GATE = COL_DT + SSM_HEADS
IN_COLS = COL_GATE + N_BRANCH * D_MODEL
PROJ_COLS = COL_DT + N_BRANCH * D_MODEL

LANES = 128
HALO = 16
VMEM_LIMIT = 56 * 1024 * 1024


def _params(*sem):
    return pltpu.CompilerParams(dimension_semantics=sem, vmem_limit_bytes=VMEM_LIMIT)


def _silu(x):
    return x * (1.0 / (1.0 + jnp.exp(-x)))


def _sigmoid(x):
    return 1.0 / (1.0 + jnp.exp(-x))


def _split3(a):
    a1 = a.astype(BF16)
    r1 = a - a1.astype(F32)
    a2 = r1.astype(BF16)
    a3 = (r1 - a2.astype(F32)).astype(BF16)
    return a1, a2, a3


def _dot_exact_rhs(a, b_exact, passes=3):
    out = None
    for piece in _split3(a)[:passes]:
        t = jnp.dot(piece, b_exact, preferred_element_type=F32)
        out = t if out is None else out + t
    return out


def _dot_exact_lhs(a_exact, b, passes=3):
    out = None
    for piece in _split3(b)[:passes]:
        t = jnp.dot(a_exact, piece, preferred_element_type=F32)
        out = t if out is None else out + t
    return out


def _head_norm_rope(x, gain, bd, cos, sin_up, sin_dn):
    ms = _dot_exact_rhs(x * x, bd, passes=2)
    y = x * lax.rsqrt(ms + NORM_EPS) * gain
    half = ROPE_DIM // 2
    return y * cos + pltpu.roll(y, half, 1) * sin_up + pltpu.roll(y, LANES - half, 1) * sin_dn


def _rope_kernel(pos_ref, inv_ref, cos_ref, up_ref, dn_ref):
    ang = pos_ref[0] * inv_ref[...]
    d = lax.broadcasted_iota(jnp.int32, ang.shape, 1) % HEAD_DIM
    half = ROPE_DIM // 2
    s = jnp.sin(ang)
    cos_ref[0] = jnp.cos(ang)
    up_ref[0] = jnp.where((d >= half) & (d < ROPE_DIM), s, 0.0)
    dn_ref[0] = jnp.where(d < half, -s, 0.0)


def rope_tables(positions):
    bsz, seq = positions.shape
    ts = min(seq, 1024)
    d = np.arange(LANES) % HEAD_DIM
    inv = ROPE_THETA ** (-jnp.arange(0, ROPE_DIM, 2, dtype=F32) / ROPE_DIM)
    inv_lane = jnp.where(d < ROPE_DIM, inv[d % (ROPE_DIM // 2)], 0.0).astype(F32)[None, :]
    pos = positions.astype(F32)[..., None]
    shp = jax.ShapeDtypeStruct((bsz, seq, LANES), F32)
    spec = pl.BlockSpec((1, ts, LANES), lambda b, t: (b, t, 0))
    return pl.pallas_call(
        _rope_kernel, out_shape=(shp, shp, shp), grid=(bsz, seq // ts),
        in_specs=[pl.BlockSpec((1, ts, 1), lambda b, t: (b, t, 0)),
                  pl.BlockSpec((1, LANES), lambda b, t: (0, 0))],
        out_specs=(spec, spec, spec), compiler_params=_params("parallel", "parallel"),
        name="rope_tables")(pos, inv_lane)


def _inproj_kernel(x_ref, g_ref, w_ref, wdt_ref, o_ref, dt_ref, u_ref):
    @pl.when(pl.program_id(1) == 0)
    def _():
        x = x_ref[...]
        ms = jnp.mean(x * x, axis=-1, keepdims=True)
        u = (x * lax.rsqrt(ms + NORM_EPS) * g_ref[...]).astype(BF16)
        u_ref[...] = u
        dt_ref[...] = jnp.dot(u, wdt_ref[...], preferred_element_type=F32)

    o_ref[...] = jnp.dot(u_ref[...], w_ref[...], preferred_element_type=F32)


def in_projection(x2, gain, w_main, w_dt, tm=1024, tn=1024):
    n, d = x2.shape
    cols = w_main.shape[1]
    tm = min(tm, n)
    return pl.pallas_call(
        _inproj_kernel,
        out_shape=(jax.ShapeDtypeStruct((n, cols), F32), jax.ShapeDtypeStruct((n, LANES), F32)),
        grid=(n // tm, cols // tn),
        in_specs=[pl.BlockSpec((tm, d), lambda i, j: (i, 0)),
                  pl.BlockSpec((1, d), lambda i, j: (0, 0)),
                  pl.BlockSpec((d, tn), lambda i, j: (0, j)),
                  pl.BlockSpec((d, LANES), lambda i, j: (0, 0))],
        out_specs=(pl.BlockSpec((tm, tn), lambda i, j: (i, j)),
                   pl.BlockSpec((tm, LANES), lambda i, j: (i, 0))),
        scratch_shapes=[pltpu.VMEM((tm, d), BF16)],
        compiler_params=_params("parallel", "arbitrary"), name="in_projection")(x2, gain, w_main, w_dt)


def _moba_kernel(q_ref, k_ref, v_ref, cos_ref, up_ref, dn_ref, gq_ref, gk_ref, bd_ref, o_ref,
                 qs_ref, ks_ref, vt_ref, qf_ref, km_ref, bias_ref, *, nb):
    blk = MOBA_BLOCK
    bd = bd_ref[...]

    def prep(j, carry):
        rows = pl.ds(pl.multiple_of(j * blk, blk), blk)
        cos, up, dn = cos_ref[0, rows, :], up_ref[0, rows, :], dn_ref[0, rows, :]
        qn = _head_norm_rope(q_ref[0, rows, :], gq_ref[...], bd, cos, up, dn)
        kn = _head_norm_rope(k_ref[0, rows, :], gk_ref[...], bd, cos, up, dn)
        qf_ref[rows, :] = qn
        km_ref[pl.ds(j, 1), :] = jnp.mean(kn, axis=0, keepdims=True)
        vt = v_ref[0, rows, :].T.astype(BF16)
        for h in range(2):
            lanes = slice(h * HEAD_DIM, (h + 1) * HEAD_DIM)
            qs_ref[h, j] = (qn[:, lanes] * HEAD_DIM ** -0.5).astype(BF16)
            ks_ref[h, j] = kn[:, lanes].astype(BF16)
            vt_ref[h, j] = vt[h * HEAD_DIM:(h + 1) * HEAD_DIM, :]
        return carry

    lax.fori_loop(0, nb, prep, 0)

    rowf = lax.broadcasted_iota(jnp.int32, (nb, blk), 0).astype(F32)
    nbf = float(nb)

    def select(i, carry):
        rows = pl.ds(pl.multiple_of(i * blk, blk), blk)
        valid = rowf < i.astype(F32)
        km, qf = km_ref[...], qf_ref[rows, :]
        for h in range(2):
            lanes = slice(h * HEAD_DIM, (h + 1) * HEAD_DIM)
            sc = lax.dot_general(km[:, lanes], qf[:, lanes], (((1,), (1,)), ((), ())),
                                 precision=lax.Precision.HIGHEST, preferred_element_type=F32)
            cur = jnp.where(valid, sc, -jnp.inf)
            sel = jnp.zeros((nb, blk), F32)
            for _ in range(min(MOBA_TOPK, nb)):
                mx = jnp.max(cur, axis=0, keepdims=True)
                first = jnp.min(jnp.where(cur == mx, rowf, nbf), axis=0, keepdims=True)
                hit = rowf == first
                sel = jnp.where(hit, 1.0, sel)
                cur = jnp.where(hit, -jnp.inf, cur)
            bias_ref[h, i] = jnp.where((sel > 0.5) & valid, 0.0, NEG_INF)
        return carry

    lax.fori_loop(0, nb, select, 0)

    kk = lax.broadcasted_iota(jnp.int32, (blk, blk), 0)
    qq = lax.broadcasted_iota(jnp.int32, (blk, blk), 1)
    causal = kk <= qq
    nt = (((1,), (1,)), ((), ()))

    def qblock(i, carry):
        outs = []
        for h in range(2):
            qb = qs_ref[h, i]
            s = lax.dot_general(ks_ref[h, i], qb, nt, preferred_element_type=F32)
            s = jnp.where(causal, s, NEG_INF)
            m = jnp.max(s, axis=0, keepdims=True)
            p = jnp.exp(s - m)
            l = jnp.sum(p, axis=0, keepdims=True)
            acc = jnp.dot(vt_ref[h, i], p.astype(BF16), preferred_element_type=F32)

            def kblock(j, c):
                m, l, acc = c
                s = lax.dot_general(ks_ref[h, j], qb, nt, preferred_element_type=F32)
                s = s + bias_ref[h, i, pl.ds(j, 1), :]
                m_new = jnp.maximum(m, jnp.max(s, axis=0, keepdims=True))
                alpha = jnp.exp(m - m_new)
                p = jnp.exp(s - m_new)
                l = alpha * l + jnp.sum(p, axis=0, keepdims=True)
                acc = alpha * acc + jnp.dot(vt_ref[h, j], p.astype(BF16), preferred_element_type=F32)
                return m_new, l, acc

            m, l, acc = lax.fori_loop(0, i, kblock, (m, l, acc))
            outs.append(acc / l)
        o_ref[0, pl.ds(pl.multiple_of(i * blk, blk), blk), :] = jnp.concatenate(outs, axis=0).T
        return carry

    lax.fori_loop(0, nb, qblock, 0)


def moba_attention(proj3, tables, gq, gk, bd):
    bsz, seq, _ = proj3.shape
    nb = seq // MOBA_BLOCK
    hp = MOBA_WIDTH // LANES
    cos, up, dn = tables
    qkv = lambda part: pl.BlockSpec((1, seq, LANES), lambda b, p, part=part: (b, 0, part * hp + p))
    tab = pl.BlockSpec((1, seq, LANES), lambda b, p: (b, 0, 0))
    vec = pl.BlockSpec((1, LANES), lambda b, p: (0, 0))
    return pl.pallas_call(
        functools.partial(_moba_kernel, nb=nb),
        out_shape=jax.ShapeDtypeStruct((bsz, seq, MOBA_WIDTH), F32), grid=(bsz, hp),
        in_specs=[qkv(0), qkv(1), qkv(2), tab, tab, tab, vec, vec,
                  pl.BlockSpec((LANES, LANES), lambda b, p: (0, 0))],
        out_specs=pl.BlockSpec((1, seq, LANES), lambda b, p: (b, 0, p)),
        scratch_shapes=[pltpu.VMEM((2, nb, MOBA_BLOCK, HEAD_DIM), BF16),
                        pltpu.VMEM((2, nb, MOBA_BLOCK, HEAD_DIM), BF16),
                        pltpu.VMEM((2, nb, HEAD_DIM, MOBA_BLOCK), BF16),
                        pltpu.VMEM((seq, LANES), F32),
                        pltpu.VMEM((nb, LANES), F32),
                        pltpu.VMEM((2, nb, nb, MOBA_BLOCK), F32)],
        compiler_params=_params("parallel", "parallel"), name="moba_attention",
    )(proj3, proj3, proj3, cos, up, dn, gq, gk, bd)


def _dilated_kernel(q_ref, k_ref, v_ref, cos_ref, up_ref, dn_ref, gq_ref, gk_ref, bd_ref,
                    o_ref, lse_ref, qs_ref, ks_ref, vs_ref, *, t_len):
    qb = DIL_QBLOCK
    nq = t_len // qb
    bd = bd_ref[...]

    def prep(j, carry):
        rows = pl.ds(pl.multiple_of(j * qb, qb), qb)
        cos, up, dn = cos_ref[0, rows, :], up_ref[0, rows, :], dn_ref[0, rows, :]
        qn = _head_norm_rope(q_ref[0, rows, :], gq_ref[...], bd, cos, up, dn)
        kn = _head_norm_rope(k_ref[0, rows, :], gk_ref[...], bd, cos, up, dn)
        v = v_ref[0, rows, :]
        for h in range(2):
            lanes = slice(h * HEAD_DIM, (h + 1) * HEAD_DIM)
            qs_ref[h, rows, :] = (qn[:, lanes] * HEAD_DIM ** -0.5).astype(BF16)
            ks_ref[h, rows, :] = kn[:, lanes].astype(BF16)
            vs_ref[h, rows, :] = v[:, lanes].astype(BF16)
        return carry

    lax.fori_loop(0, nq, prep, 0)

    rel = (lax.broadcasted_iota(jnp.int32, (qb, 2 * qb), 0)
           - lax.broadcasted_iota(jnp.int32, (qb, 2 * qb), 1))
    nt = (((1,), (1,)), ((), ()))

    def qblock(i, carry):
        k0 = jnp.maximum(i - 1, 0) * qb
        krows = pl.ds(pl.multiple_of(k0, qb), 2 * qb)
        qrows = pl.ds(pl.multiple_of(i * qb, qb), qb)
        dist = rel + (i * qb - k0)
        ok = (dist >= 0) & (dist <= DIL_WINDOW)
        outs, lses = [], []
        for h in range(2):
            s = lax.dot_general(qs_ref[h, qrows, :], ks_ref[h, krows, :], nt,
                                preferred_element_type=F32)
            s = jnp.where(ok, s, NEG_INF)
            m = jnp.max(s, axis=-1, keepdims=True)
            e = jnp.exp(s - m)
            den = jnp.sum(e, axis=-1, keepdims=True)
            p = (e / den).astype(BF16)
            outs.append(jnp.dot(p, vs_ref[h, krows, :], preferred_element_type=F32))
            lses.append(jnp.broadcast_to(m + jnp.log(den), (qb, HEAD_DIM)))
        o_ref[0, qrows, :] = jnp.concatenate(outs, axis=1)
        lse_ref[0, qrows, :] = jnp.concatenate(lses, axis=1)
        return carry

    lax.fori_loop(0, nq, qblock, 0)


def dilated_group(proj3, tables, gq, gk, bd, group):
    bsz, seq, cols = proj3.shape
    rate = DIL_RATES[group]
    t_len = seq // rate
    hp = DIL_WIDTH // LANES
    cb = cols // LANES
    base = COL_DIL // LANES
    view = proj3.reshape(bsz, t_len, rate * cols)
    tabs = [t.reshape(bsz, t_len, rate * LANES) for t in tables]
    qkv = lambda part: pl.BlockSpec(
        (1, t_len, LANES), lambda b, c, p, part=part: (b, 0, c * cb + base + (part * DIL_GROUPS + group) * hp + p))
    tab = pl.BlockSpec((1, t_len, LANES), lambda b, c, p: (b, 0, c))
    vec = pl.BlockSpec((1, LANES), lambda b, c, p: (0, 0))
    out = pl.BlockSpec((1, t_len, LANES), lambda b, c, p: (b, 0, c * hp + p))
    shp = jax.ShapeDtypeStruct((bsz, t_len, rate * DIL_WIDTH), F32)
    o, lse = pl.pallas_call(
        functools.partial(_dilated_kernel, t_len=t_len), out_shape=(shp, shp), grid=(bsz, rate, hp),
        in_specs=[qkv(0), qkv(1), qkv(2), tab, tab, tab, vec, vec,
                  pl.BlockSpec((LANES, LANES), lambda b, c, p: (0, 0))],
        out_specs=(out, out),
        scratch_shapes=[pltpu.VMEM((2, t_len, HEAD_DIM), BF16)] * 3,
        compiler_params=_params("parallel", "parallel", "parallel"), name=f"dilated_rate{rate}",
    )(view, view, view, *tabs, gq, gk, bd)
    return o.reshape(bsz, seq, DIL_WIDTH), lse.reshape(bsz, seq, DIL_WIDTH)


def _ssd_kernel(z_ref, xs_ref, bc_ref, dt_ref, cwx_ref, cwb_ref, cbx_ref, cbb_ref, dtb_ref, alog_ref,
                dexp_ref, onorm_ref, tri_ref, triu_ref, exp_ref, o_ref, xpx_ref, xpb_ref, st_ref):
    L = SSM_CHUNK
    pad = 8

    @pl.when(pl.program_id(1) == 0)
    def _():
        xpx_ref[0:pad, :] = jnp.zeros((pad, SSM_INNER), F32)
        xpb_ref[0:pad, :] = jnp.zeros((pad, SSM_INNER), F32)
        st_ref[...] = jnp.zeros(st_ref.shape, F32)

    def conv_silu(src_ref, pad_ref, w_ref, b_ref):
        pad_ref[pad:, :] = src_ref[0]
        acc = b_ref[...] + w_ref[0:1, :] * pad_ref[pl.ds(pad - SSM_CONV + 1, L), :]
        for k in range(1, SSM_CONV):
            acc = acc + w_ref[k:k + 1, :] * pad_ref[pl.ds(pad - SSM_CONV + 1 + k, L), :]
        pad_ref[0:pad, :] = pad_ref[L:L + pad, :]
        return _silu(acc)

    xs = conv_silu(xs_ref, xpx_ref, cwx_ref, cbx_ref)
    bc = conv_silu(bc_ref, xpb_ref, cwb_ref, cbb_ref)
    gn = SSM_GROUPS * SSM_STATE
    bm, cm = bc[:, :gn], bc[:, gn:]

    xr = dt_ref[0] + dtb_ref[...]
    dt = jnp.maximum(xr, 0.0) + jnp.log(1.0 + jnp.exp(-jnp.abs(xr)))
    adt = dt * (-jnp.exp(alog_ref[...]))
    acs = _dot_exact_lhs(tri_ref[...], adt)
    acs_t = _dot_exact_rhs(adt.T, triu_ref[...])
    expand = exp_ref[...]
    dt_e = _dot_exact_rhs(dt, expand)
    acs_e = _dot_exact_rhs(acs, expand)
    xdt = xs * dt_e
    last = acs_e[L - 1:L, :]
    grow = jnp.exp(acs_e)
    to_end = jnp.exp(last - acs_e)
    chunk_decay = jnp.exp(last)

    ll = lax.broadcasted_iota(jnp.int32, (L, L), 0)
    ss = lax.broadcasted_iota(jnp.int32, (L, L), 1)
    causal = ll >= ss
    gw = SSM_INNER // SSM_GROUPS
    hpg = SSM_HEADS // SSM_GROUPS
    lane = lax.broadcasted_iota(jnp.int32, (L, gw), 1)
    nt = (((1,), (1,)), ((), ()))
    tn = (((0,), (0,)), ((), ()))
    ys = []
    for g in range(SSM_GROUPS):
        cols = slice(g * gw, (g + 1) * gw)
        bg = bm[:, g * SSM_STATE:(g + 1) * SSM_STATE].astype(BF16)
        cg = cm[:, g * SSM_STATE:(g + 1) * SSM_STATE].astype(BF16)
        xg = xdt[:, cols]
        cb = lax.dot_general(cg, bg, nt, preferred_element_type=F32)
        st = st_ref[g]
        y = jnp.dot(cg, st.astype(BF16), preferred_element_type=F32) * grow[:, cols]
        new = lax.dot_general(bg, (xg * to_end[:, cols]).astype(BF16), tn, preferred_element_type=F32)
        st_ref[g] = chunk_decay[:, cols] * st + new
        for hh in range(hpg):
            h = g * hpg + hh
            diff = acs[:, h:h + 1] - acs_t[h:h + 1, :]
            mat = (cb * jnp.exp(jnp.where(causal, diff, -jnp.inf))).astype(BF16)
            xh = jnp.where((lane >= hh * SSM_HEAD_DIM) & (lane < (hh + 1) * SSM_HEAD_DIM), xg, 0.0)
            y = y + jnp.dot(mat, xh.astype(BF16), preferred_element_type=F32)
        ys.append(y)
    y = jnp.concatenate(ys, axis=1) + xs * dexp_ref[...]
    yg = y * _silu(z_ref[0])
    ms = jnp.mean(yg * yg, axis=-1, keepdims=True)
    o_ref[0] = yg * lax.rsqrt(ms + NORM_EPS) * onorm_ref[...]


def ssd_mixer(proj3, dt3, conv_w, conv_b, dt_bias, a_log, d_exp, out_norm, consts):
    bsz, seq, _ = proj3.shape
    L = SSM_CHUNK
    w = SSM_INNER
    tri, triu, expand = consts
    col = lambda idx: pl.BlockSpec((1, L, w), lambda b, c, idx=idx: (b, c, idx))
    vecw = lambda rows, idx: pl.BlockSpec((rows, w), lambda b, c, idx=idx: (0, idx))
    vec = pl.BlockSpec((1, LANES), lambda b, c: (0, 0))
    sq = pl.BlockSpec((L, L), lambda b, c: (0, 0))
    return pl.pallas_call(
        _ssd_kernel, out_shape=jax.ShapeDtypeStruct((bsz, seq, w), F32), grid=(bsz, seq // L),
        in_specs=[col(COL_Z // w), col(COL_XBC // w), col(COL_XBC // w + 1),
                  pl.BlockSpec((1, L, LANES), lambda b, c: (b, c, 0)),
                  vecw(SSM_CONV, 0), vecw(SSM_CONV, 1), vecw(1, 0), vecw(1, 1), vec, vec,
                  vecw(1, 0), vecw(1, 0), sq, sq, pl.BlockSpec((LANES, w), lambda b, c: (0, 0))],
        out_specs=pl.BlockSpec((1, L, w), lambda b, c: (b, c, 0)),
        scratch_shapes=[pltpu.VMEM((L + 8, w), F32), pltpu.VMEM((L + 8, w), F32),
                        pltpu.VMEM((SSM_GROUPS, SSM_STATE, w // SSM_GROUPS), F32)],
        compiler_params=_params("parallel", "arbitrary"), name="ssd_mixer",
    )(proj3, proj3, proj3, dt3, conv_w, conv_w, conv_b, conv_b, dt_bias, a_log, d_exp, out_norm,
      tri, triu, expand)


def _merge_kernel(x_ref, a_ref, m_ref, o0_ref, o1_ref, o2_ref, l0_ref, l1_ref, l2_ref, gl_ref, bg_ref,
                  wa_ref, wm_ref, wc_ref, wo_ref, out_ref):
    l0, l1, l2 = l0_ref[...], l1_ref[...], l2_ref[...]
    lmax = jnp.maximum(jnp.maximum(l0, l1), l2)
    e0, e1, e2 = jnp.exp(l0 - lmax), jnp.exp(l1 - lmax), jnp.exp(l2 - lmax)
    cmix = (e0 * o0_ref[...] + e1 * o1_ref[...] + e2 * o2_ref[...]) / (e0 + e1 + e2)
    gates = _sigmoid(gl_ref[...] + bg_ref[...])
    d = D_MODEL
    mm = lambda v, w_ref: jnp.dot(v.astype(BF16), w_ref[...], preferred_element_type=F32)
    merged = (gates[:, :d] * mm(a_ref[...], wa_ref) + gates[:, d:2 * d] * mm(m_ref[...], wm_ref)
              + gates[:, 2 * d:] * mm(cmix, wc_ref))
    out_ref[...] = x_ref[...] + mm(merged, wo_ref)


def merge_branches(x2, a2, m2, dil, proj2, b_gate, wa, wm, wc, wo, tm=256):
    n, d = x2.shape
    tm = min(tm, n)
    row = lambda width, idx=0: pl.BlockSpec((tm, width), lambda i, idx=idx: (i, idx))
    full = lambda arr: pl.BlockSpec(arr.shape, lambda i: (0, 0))
    (o0, l0), (o1, l1), (o2, l2) = dil
    gw = N_BRANCH * d
    return pl.pallas_call(
        _merge_kernel, out_shape=jax.ShapeDtypeStruct((n, d), F32), grid=(n // tm,),
        in_specs=[row(d), row(MOBA_WIDTH), row(SSM_INNER)] + [row(DIL_WIDTH)] * 6
                 + [row(gw, COL_DT // gw), full(b_gate), full(wa), full(wm), full(wc), full(wo)],
        out_specs=row(d), compiler_params=_params("parallel"), name="merge_branches",
    )(x2, a2, m2, o0, o1, o2, l0, l1, l2, proj2, b_gate, wa, wm, wc, wo)


def _ffn_kernel(x_ref, xh_ref, g_ref, wa_ref, wb_ref, cwa_ref, cwb_ref, cba_ref, cbb_ref, wd_ref, o_ref,
                u_ref, ua_ref, ub_ref, *, tm, tiles_per_seq):
    n = pl.program_id(1)

    def norm(x):
        ms = jnp.mean(x * x, axis=-1, keepdims=True)
        return (x * lax.rsqrt(ms + NORM_EPS) * g_ref[...]).astype(BF16)

    @pl.when(n == 0)
    def _():
        u_ref[0:HALO, :] = norm(xh_ref[...])
        u_ref[HALO:, :] = norm(x_ref[...])
        o_ref[...] = x_ref[...]

    keep = jnp.where(pl.program_id(0) % tiles_per_seq == 0, 0.0, 1.0)
    u = u_ref[...]

    def conv(w_ref, cw_ref, cb_ref, buf_ref):
        buf_ref[...] = jnp.dot(u, w_ref[...], preferred_element_type=F32)
        buf_ref[0:HALO, :] = buf_ref[0:HALO, :] * keep
        acc = cb_ref[...] + cw_ref[0:1, :] * buf_ref[pl.ds(HALO - FFN_CONV + 1, tm), :]
        for k in range(1, FFN_CONV):
            acc = acc + cw_ref[k:k + 1, :] * buf_ref[pl.ds(HALO - FFN_CONV + 1 + k, tm), :]
        return acc

    ga = conv(wa_ref, cwa_ref, cba_ref, ua_ref)
    gb = conv(wb_ref, cwb_ref, cbb_ref, ub_ref)
    h = (_silu(ga) * gb).astype(BF16)
    o_ref[...] += jnp.dot(h, wd_ref[...], preferred_element_type=F32)


def ffn(x2, gain, w_up, conv_w, conv_b, w_down, seq, tm=1024):
    n, d = x2.shape
    tm = min(tm, seq)
    ck = FFN_CHUNK
    nck = FFN_DIM // ck
    hb = tm // HALO
    return pl.pallas_call(
        functools.partial(_ffn_kernel, tm=tm, tiles_per_seq=seq // tm),
        out_shape=jax.ShapeDtypeStruct((n, d), F32), grid=(n // tm, nck),
        in_specs=[pl.BlockSpec((tm, d), lambda i, c: (i, 0)),
                  pl.BlockSpec((HALO, d), lambda i, c: (jnp.maximum(i * hb - 1, 0), 0)),
                  pl.BlockSpec((1, d), lambda i, c: (0, 0)),
                  pl.BlockSpec((d, ck), lambda i, c: (0, c)),
                  pl.BlockSpec((d, ck), lambda i, c: (0, nck + c)),
                  pl.BlockSpec((FFN_CONV, ck), lambda i, c: (0, c)),
                  pl.BlockSpec((FFN_CONV, ck), lambda i, c: (0, nck + c)),
                  pl.BlockSpec((1, ck), lambda i, c: (0, c)),
                  pl.BlockSpec((1, ck), lambda i, c: (0, nck + c)),
                  pl.BlockSpec((ck, d), lambda i, c: (c, 0))],
        out_specs=pl.BlockSpec((tm, d), lambda i, c: (i, 0)),
        scratch_shapes=[pltpu.VMEM((tm + HALO, d), BF16), pltpu.VMEM((tm + HALO, ck), F32),
                        pltpu.VMEM((tm + HALO, ck), F32)],
        compiler_params=_params("parallel", "arbitrary"), name="ffn",
    )(x2, x2, gain, w_up, w_up, conv_w, conv_w, conv_b, conv_b, w_down)


def _ple_kernel(x_ref, p_ref, g_ref, wg_ref, wp_ref, o_ref):
    x = x_ref[...]
    ms = jnp.mean(x * x, axis=-1, keepdims=True)
    u = (x * lax.rsqrt(ms + NORM_EPS) * g_ref[...]).astype(BF16)
    pg = _sigmoid(jnp.dot(u, wg_ref[...], preferred_element_type=F32))
    o_ref[...] = x + jnp.dot(p_ref[...].astype(BF16), wp_ref[...], preferred_element_type=F32) * pg


def ple(x2, p2, gain, w_gate, w_ple, tm=512):
    n, d = x2.shape
    tm = min(tm, n)
    return pl.pallas_call(
        _ple_kernel, out_shape=jax.ShapeDtypeStruct((n, d), F32), grid=(n // tm,),
        in_specs=[pl.BlockSpec((tm, d), lambda i: (i, 0)), pl.BlockSpec((tm, PLE_DIM), lambda i: (i, 0)),
                  pl.BlockSpec((1, d), lambda i: (0, 0)), pl.BlockSpec((d, d), lambda i: (0, 0)),
                  pl.BlockSpec((PLE_DIM, d), lambda i: (0, 0))],
        out_specs=pl.BlockSpec((tm, d), lambda i: (i, 0)),
        compiler_params=_params("parallel"), name="ple")(x2, p2, gain, w_gate, w_ple)


def _constants():
    lane = np.arange(LANES)
    bd = (lane[:, None] // HEAD_DIM == lane[None, :] // HEAD_DIM).astype(np.float32) / HEAD_DIM
    r = np.arange(SSM_CHUNK)
    tri = (r[None, :] <= r[:, None]).astype(np.float32)
    expand = (lane[:, None] == (np.arange(SSM_INNER)[None, :] // SSM_HEAD_DIM)).astype(np.float32)
    as_bf16 = lambda a: jnp.asarray(a, dtype=BF16)
    return as_bf16(bd), (as_bf16(tri), as_bf16(tri.T), as_bf16(expand))


def _pad_lanes(v):
    return jnp.pad(v, (0, LANES - v.shape[0]))[None, :]


def kernel(x, p, positions, norm_mix, w_in, b_gate, moba_q_norm, moba_k_norm, dil_q_norm, dil_k_norm,
           ssm_conv_w, ssm_conv_b, ssm_dt_bias, ssm_a_log, ssm_d, ssm_out_norm, w_br_moba, w_br_ssm,
           w_br_dil, w_out, norm_ffn, w_up, ffn_conv_w, ffn_conv_b, w_down, norm_ple, w_ple_gate, w_ple):
    bsz, seq, d = x.shape
    depth = w_in.shape[0]
    n = bsz * seq
    bd, ssd_consts = _constants()
    tables = rope_tables(positions)
    row = lambda v: v[None, :]
    two = lambda v: jnp.tile(v, 2)[None, :]

    x2 = x.reshape(n, d)
    for i in range(depth):
        w_main = jnp.concatenate([w_in[i][:, :COL_DT], w_in[i][:, COL_GATE:]], axis=1).astype(BF16)
        w_dt = jnp.pad(w_in[i][:, COL_DT:COL_GATE], ((0, 0), (0, LANES - SSM_HEADS))).astype(BF16)
        proj2, dt2 = in_projection(x2, row(norm_mix[i]), w_main, w_dt)
        proj3 = proj2.reshape(bsz, seq, PROJ_COLS)

        out_a = moba_attention(proj3, tables, two(moba_q_norm[i]), two(moba_k_norm[i]), bd)
        out_b = ssd_mixer(proj3, dt2.reshape(bsz, seq, LANES), ssm_conv_w[i], row(ssm_conv_b[i]),
                          _pad_lanes(ssm_dt_bias[i]), _pad_lanes(ssm_a_log[i]),
                          row(jnp.repeat(ssm_d[i], SSM_HEAD_DIM)), row(ssm_out_norm[i]), ssd_consts)
        dil = [dilated_group(proj3, tables, two(dil_q_norm[i]), two(dil_k_norm[i]), bd, g)
               for g in range(DIL_GROUPS)]
        dil2 = [(o.reshape(n, DIL_WIDTH), l.reshape(n, DIL_WIDTH)) for o, l in dil]

        x2 = merge_branches(x2, out_a.reshape(n, MOBA_WIDTH), out_b.reshape(n, SSM_INNER), dil2, proj2,
                            row(b_gate[i]), w_br_moba[i].astype(BF16), w_br_ssm[i].astype(BF16),
                            w_br_dil[i].astype(BF16), w_out[i].astype(BF16))
        x2 = ffn(x2, row(norm_ffn[i]), w_up[i].astype(BF16), ffn_conv_w[i], row(ffn_conv_b[i]),
                 w_down[i].astype(BF16), seq)
        x2 = ple(x2, p[i].reshape(n, PLE_DIM), row(norm_ple[i]), w_ple_gate[i].astype(BF16),
                 w_ple[i].astype(BF16))
    return x2.reshape(bsz, seq, d)
```

```python
import functools
import math

import numpy as np
import jax
import jax.numpy as jnp
from jax import lax
from jax.experimental import pallas as pl
from jax.experimental.pallas import tpu as pltpu

F32 = jnp.float32
BF16 = jnp.bfloat16

D_MODEL = 1024
PLE_DIM = 256
HEAD_DIM = 64
ROPE_DIM = HEAD_DIM // 4
ROPE_THETA = 500000.0
NORM_EPS = 1e-6
NEG_INF = -1e30

MOBA_HEADS = 8
MOBA_BLOCK = 256
MOBA_TOPK = 3
MOBA_WIDTH = MOBA_HEADS * HEAD_DIM

DIL_RATES = (1, 4, 16)
DIL_GROUPS = 3
DIL_HEADS = 8
DIL_WINDOW = 128
DIL_WIDTH = DIL_HEADS * HEAD_DIM
DIL_QBLOCK = 128
DIL_UNROLL = 4

SSM_INNER = D_MODEL
SSM_HEAD_DIM = 64
SSM_HEADS = SSM_INNER // SSM_HEAD_DIM
SSM_GROUPS = 4
SSM_STATE = 128
SSM_CONV = 4
SSM_CHUNK = 128
SSM_XBC = SSM_INNER + 2 * SSM_GROUPS * SSM_STATE

FFN_DIM = 2816
FFN_CONV = 3
FFN_CHUNK = 256
N_BRANCH = 3

COL_MOBA = 0
COL_DIL = 3 * MOBA_WIDTH
COL_Z = COL_DIL + 3 * DIL_GROUPS * DIL_WIDTH
COL_XBC = COL_Z + SSM_INNER
COL_DT = COL_XBC + SSM_XBC
COL_GATE = COL_DT + SSM_HEADS
IN_COLS = COL_GATE + N_BRANCH * D_MODEL
PROJ_COLS = COL_DT + N_BRANCH * D_MODEL

QSCALE = HEAD_DIM ** -0.5 * math.log2(math.e)

LANES = 128
HALO = 16
VMEM_LIMIT = 56 * 1024 * 1024


def _params(*sem):
    return pltpu.CompilerParams(dimension_semantics=sem, vmem_limit_bytes=VMEM_LIMIT)


def _silu(x):
    return x * (1.0 / (1.0 + jnp.exp(-x)))


def _sigmoid(x):
    return 1.0 / (1.0 + jnp.exp(-x))


def _split3(a):
    a1 = a.astype(BF16)
    r1 = a - a1.astype(F32)
    a2 = r1.astype(BF16)
    a3 = (r1 - a2.astype(F32)).astype(BF16)
    return a1, a2, a3


def _dot_exact_rhs(a, b_exact, passes=3):
    out = None
    for piece in _split3(a)[:passes]:
        t = jnp.dot(piece, b_exact, preferred_element_type=F32)
        out = t if out is None else out + t
    return out


def _dot_exact_lhs(a_exact, b, passes=3):
    out = None
    for piece in _split3(b)[:passes]:
        t = jnp.dot(a_exact, piece, preferred_element_type=F32)
        out = t if out is None else out + t
    return out


def _head_norm_rope(x, gain, bd, cos, sin_up, sin_dn):
    ms = _dot_exact_rhs(x * x, bd, passes=2)
    y = x * lax.rsqrt(ms + NORM_EPS) * gain
    half = ROPE_DIM // 2
    return y * cos + pltpu.roll(y, half, 1) * sin_up + pltpu.roll(y, LANES - half, 1) * sin_dn


def _rope_kernel(pos_ref, inv_ref, cos_ref, up_ref, dn_ref):
    ang = pos_ref[0] * inv_ref[...]
    d = lax.broadcasted_iota(jnp.int32, ang.shape, 1) % HEAD_DIM
    half = ROPE_DIM // 2
    s = jnp.sin(ang)
    cos_ref[0] = jnp.cos(ang)
    up_ref[0] = jnp.where((d >= half) & (d < ROPE_DIM), s, 0.0)
    dn_ref[0] = jnp.where(d < half, -s, 0.0)


def rope_tables(positions):
    bsz, seq = positions.shape
    ts = min(seq, 1024)
    d = np.arange(LANES) % HEAD_DIM
    inv = ROPE_THETA ** (-jnp.arange(0, ROPE_DIM, 2, dtype=F32) / ROPE_DIM)
    inv_lane = jnp.where(d < ROPE_DIM, inv[d % (ROPE_DIM // 2)], 0.0).astype(F32)[None, :]
    pos = positions.astype(F32)[..., None]
    shp = jax.ShapeDtypeStruct((bsz, seq, LANES), F32)
    spec = pl.BlockSpec((1, ts, LANES), lambda b, t: (b, t, 0))
    return pl.pallas_call(
        _rope_kernel, out_shape=(shp, shp, shp), grid=(bsz, seq // ts),
        in_specs=[pl.BlockSpec((1, ts, 1), lambda b, t: (b, t, 0)),
                  pl.BlockSpec((1, LANES), lambda b, t: (0, 0))],
        out_specs=(spec, spec, spec), compiler_params=_params("parallel", "parallel"),
        name="rope_tables")(pos, inv_lane)


def _inproj_kernel(x_ref, g_ref, w_ref, wdt_ref, o_ref, dt_ref, u_ref):
    @pl.when(pl.program_id(1) == 0)
    def _():
        x = x_ref[...]
        ms = jnp.mean(x * x, axis=-1, keepdims=True)
        u = (x * lax.rsqrt(ms + NORM_EPS) * g_ref[...]).astype(BF16)
        u_ref[...] = u
        dt_ref[...] = jnp.dot(u, wdt_ref[...], preferred_element_type=F32)

    o_ref[...] = jnp.dot(u_ref[...], w_ref[...], preferred_element_type=F32)


def in_projection(x2, gain, w_main, w_dt, tm=1024, tn=1024):
    n, d = x2.shape
    cols = w_main.shape[1]
    tm = min(tm, n)
    return pl.pallas_call(
        _inproj_kernel,
        out_shape=(jax.ShapeDtypeStruct((n, cols), F32), jax.ShapeDtypeStruct((n, LANES), F32)),
        grid=(n // tm, cols // tn),
        in_specs=[pl.BlockSpec((tm, d), lambda i, j: (i, 0)),
                  pl.BlockSpec((1, d), lambda i, j: (0, 0)),
                  pl.BlockSpec((d, tn), lambda i, j: (0, j)),
                  pl.BlockSpec((d, LANES), lambda i, j: (0, 0))],
        out_specs=(pl.BlockSpec((tm, tn), lambda i, j: (i, j)),
                   pl.BlockSpec((tm, LANES), lambda i, j: (i, 0))),
        scratch_shapes=[pltpu.VMEM((tm, d), BF16)],
        compiler_params=_params("parallel", "arbitrary"), name="in_projection")(x2, gain, w_main, w_dt)


def _head_lanes(shape):
    lane = lax.broadcasted_iota(jnp.int32, shape, len(shape) - 1)
    return lane, (lane < HEAD_DIM, lane >= HEAD_DIM), (HEAD_DIM, 0)


def _moba_kernel(q_ref, k_ref, v_ref, cos_ref, up_ref, dn_ref, gq_ref, gk_ref, bd_ref, o_ref,
                 qa_ref, ka_ref, va_ref, qf_ref, km_ref, *, nb):
    blk = MOBA_BLOCK
    bd = bd_ref[...]
    lane, head, aux = _head_lanes((blk, LANES))
    km_ref[...] = jnp.zeros(km_ref.shape, F32)

    def prep(j, carry):
        rows = pl.ds(pl.multiple_of(j * blk, blk), blk)
        cos, up, dn = cos_ref[0, rows, :], up_ref[0, rows, :], dn_ref[0, rows, :]
        qf_ref[rows, :] = _head_norm_rope(q_ref[0, rows, :], gq_ref[...], bd, cos, up, dn)
        kn = _head_norm_rope(k_ref[0, rows, :], gk_ref[...], bd, cos, up, dn)
        v = v_ref[0, rows, :]
        km_ref[pl.ds(HEAD_DIM + j, 1), :] = jnp.mean(kn, axis=0, keepdims=True)
        for h in range(2):
            ka_ref[h, rows, :] = jnp.where(head[h], kn, jnp.where(lane == aux[h] + j, 1.0, 0.0)).astype(BF16)
            va_ref[h, rows, :] = jnp.where(head[h], v, jnp.where(lane == aux[h], 1.0, 0.0)).astype(BF16)
        return carry

    lax.fori_loop(0, nb, prep, 0)

    row = lax.broadcasted_iota(jnp.int32, (LANES, blk), 0)
    lane_k, head_k, _ = _head_lanes((LANES, LANES))
    big = float(LANES)

    def select(i, carry):
        rows = pl.ds(pl.multiple_of(i * blk, blk), blk)
        qf = qf_ref[rows, :]
        i_f = i.astype(F32)
        for h in range(2):
            kmh = jnp.where(head_k[h], km_ref[HEAD_DIM * h:HEAD_DIM * h + LANES, :], 0.0)
            sc = lax.dot_general(kmh, qf, (((1,), (1,)), ((), ())),
                                 precision=lax.Precision.HIGHEST, preferred_element_type=F32)
            bidx = (row - aux[h]).astype(F32)
            isblk = (bidx >= 0.0) & (bidx < float(nb))
            valid = isblk & (bidx < i_f)
            cur = jnp.where(valid, sc, -jnp.inf)
            sel = jnp.zeros((LANES, blk), F32)
            for _ in range(min(MOBA_TOPK, nb)):
                mx = jnp.max(cur, axis=0, keepdims=True)
                first = jnp.min(jnp.where((cur == mx) & isblk, bidx, big), axis=0, keepdims=True)
                hit = bidx == first
                sel = jnp.where(hit, 1.0, sel)
                cur = jnp.where(hit, -jnp.inf, cur)
            keep = ((sel > 0.5) & valid) | (bidx == i_f)
            bias_t = jnp.where(isblk & jnp.logical_not(keep), NEG_INF, 0.0).T
            qa_ref[h, rows, :] = jnp.where(head[h], qf * QSCALE, bias_t).astype(BF16)
        return carry

    lax.fori_loop(0, nb, select, 0)

    wide = 2 * blk
    _, head_w, _ = _head_lanes((wide, LANES))
    causal = (lax.broadcasted_iota(jnp.int32, (wide, wide), 1)
              <= lax.broadcasted_iota(jnp.int32, (wide, wide), 0))
    nt = (((1,), (1,)), ((), ()))

    def qblock(a, carry):
        rows = pl.ds(pl.multiple_of(a * wide, wide), wide)
        qas = [qa_ref[h, rows, :] for h in range(2)]
        state = []
        for h in range(2):
            s = lax.dot_general(qas[h], ka_ref[h, rows, :], nt, preferred_element_type=F32)
            s = jnp.where(causal, s, NEG_INF)
            m = jnp.max(s, axis=-1, keepdims=True)
            p = jnp.exp2(s - m)
            state += [m, jnp.dot(p.astype(BF16), va_ref[h, rows, :], preferred_element_type=F32)]

        def pair(g, c):
            krows = pl.ds(pl.multiple_of(g * wide, wide), wide)
            out = []
            for h in range(2):
                m, acc = c[2 * h], c[2 * h + 1]
                s = lax.dot_general(qas[h], ka_ref[h, krows, :], nt, preferred_element_type=F32)
                m_new = jnp.maximum(m, jnp.max(s, axis=-1, keepdims=True))
                p = jnp.exp2(s - m_new)
                acc = jnp.exp2(m - m_new) * acc + jnp.dot(p.astype(BF16), va_ref[h, krows, :],
                                                          preferred_element_type=F32)
                out += [m_new, acc]
            return tuple(out)

        state = lax.fori_loop(0, a, pair, tuple(state))
        den0 = state[1][:, aux[0]:aux[0] + 1]
        den1 = state[3][:, aux[1]:aux[1] + 1]
        o_ref[0, rows, :] = jnp.where(head_w[0], state[1] / den0, state[3] / den1)
        return carry

    lax.fori_loop(0, nb // 2, qblock, 0)


def moba_attention(proj3, tables, gq, gk, bd):
    bsz, seq, _ = proj3.shape
    nb = seq // MOBA_BLOCK
    assert nb % 2 == 0 and nb <= HEAD_DIM, "key blocks are visited in pairs and indexed on 64 spare lanes"
    hp = MOBA_WIDTH // LANES
    cos, up, dn = tables
    qkv = lambda part: pl.BlockSpec((1, seq, LANES), lambda b, p, part=part: (b, 0, part * hp + p))
    tab = pl.BlockSpec((1, seq, LANES), lambda b, p: (b, 0, 0))
    vec = pl.BlockSpec((1, LANES), lambda b, p: (0, 0))
    return pl.pallas_call(
        functools.partial(_moba_kernel, nb=nb),
        out_shape=jax.ShapeDtypeStruct((bsz, seq, MOBA_WIDTH), F32), grid=(bsz, hp),
        in_specs=[qkv(0), qkv(1), qkv(2), tab, tab, tab, vec, vec,
                  pl.BlockSpec((LANES, LANES), lambda b, p: (0, 0))],
        out_specs=pl.BlockSpec((1, seq, LANES), lambda b, p: (b, 0, p)),
        scratch_shapes=[pltpu.VMEM((2, seq, LANES), BF16), pltpu.VMEM((2, seq, LANES), BF16),
                        pltpu.VMEM((2, seq, LANES), BF16), pltpu.VMEM((seq, LANES), F32),
                        pltpu.VMEM((HEAD_DIM + LANES, LANES), F32)],
        compiler_params=_params("parallel", "parallel"), name="moba_attention",
    )(proj3, proj3, proj3, cos, up, dn, gq, gk, bd)


def _dilated_kernel(q_ref, k_ref, v_ref, cos_ref, up_ref, dn_ref, gq_ref, gk_ref, bd_ref,
                    o_ref, lse_ref, qd_ref, kd_ref, va_ref, *, seq, rate):
    qb = DIL_QBLOCK
    cpb = seq // rate // qb
    shift = cpb.bit_length() - 1
    bd = bd_ref[...]
    lane, head, aux = _head_lanes((qb, LANES))

    def token_rows(n):
        if rate == 1:
            return pl.ds(pl.multiple_of(n * qb, qb), qb)
        c, ch = lax.shift_right_logical(n, shift), n & (cpb - 1)
        return pl.ds(c + ch * (qb * rate), qb, stride=rate)

    def prep(n):
        src = token_rows(n)
        dst = pl.ds(pl.multiple_of(n * qb, qb), qb)
        cos, up, dn = cos_ref[0, src, :], up_ref[0, src, :], dn_ref[0, src, :]
        qn = _head_norm_rope(q_ref[0, src, :], gq_ref[...], bd, cos, up, dn)
        kd_ref[dst, :] = _head_norm_rope(k_ref[0, src, :], gk_ref[...], bd, cos, up, dn).astype(BF16)
        v = v_ref[0, src, :]
        for h in range(2):
            qd_ref[h, dst, :] = jnp.where(head[h], qn * QSCALE, 0.0).astype(BF16)
            va_ref[h, dst, :] = jnp.where(head[h], v, jnp.where(lane == aux[h], 1.0, 0.0)).astype(BF16)

    def unrolled(fn):
        def step(it, carry):
            for u in range(DIL_UNROLL):
                fn(it * DIL_UNROLL + u)
            return carry
        lax.fori_loop(0, seq // qb // DIL_UNROLL, step, 0)

    unrolled(prep)

    rel = (lax.broadcasted_iota(jnp.int32, (qb, 2 * qb), 0)
           - lax.broadcasted_iota(jnp.int32, (qb, 2 * qb), 1))
    nt = (((1,), (1,)), ((), ()))

    def qblock(n):
        first = (n & (cpb - 1)) == 0
        k0 = jnp.where(first, n, n - 1) * qb
        krows = pl.ds(pl.multiple_of(k0, qb), 2 * qb)
        qrows = pl.ds(pl.multiple_of(n * qb, qb), qb)
        dist = rel + (n * qb - k0)
        ok = (dist >= 0) & (dist <= DIL_WINDOW)
        res = []
        for h in range(2):
            s = lax.dot_general(qd_ref[h, qrows, :], kd_ref[krows, :], nt, preferred_element_type=F32)
            s = jnp.where(ok, s, NEG_INF)
            m = jnp.max(s, axis=-1, keepdims=True)
            p = jnp.exp2(s - m)
            res.append((m, jnp.dot(p.astype(BF16), va_ref[h, krows, :], preferred_element_type=F32)))
        den0 = res[0][1][:, aux[0]:aux[0] + 1]
        den1 = res[1][1][:, aux[1]:aux[1] + 1]
        dst = token_rows(n)
        o_ref[0, dst, :] = jnp.where(head[0], res[0][1] / den0, res[1][1] / den1)
        lse_ref[0, dst, :] = jnp.where(head[0], res[0][0] + jnp.log2(den0), res[1][0] + jnp.log2(den1))

    unrolled(qblock)


def dilated_group(proj3, tables, gq, gk, bd, group):
    bsz, seq, cols = proj3.shape
    rate = DIL_RATES[group]
    cpb = seq // rate // DIL_QBLOCK
    assert cpb >= 2 and cpb & (cpb - 1) == 0 and (seq // DIL_QBLOCK) % DIL_UNROLL == 0
    hp = DIL_WIDTH // LANES
    base = COL_DIL // LANES
    qkv = lambda part: pl.BlockSpec(
        (1, seq, LANES), lambda b, p, part=part: (b, 0, base + (part * DIL_GROUPS + group) * hp + p))
    tab = pl.BlockSpec((1, seq, LANES), lambda b, p: (b, 0, 0))
    vec = pl.BlockSpec((1, LANES), lambda b, p: (0, 0))
    out = pl.BlockSpec((1, seq, LANES), lambda b, p: (b, 0, p))
    shp = jax.ShapeDtypeStruct((bsz, seq, DIL_WIDTH), F32)
    return pl.pallas_call(
        functools.partial(_dilated_kernel, seq=seq, rate=rate), out_shape=(shp, shp), grid=(bsz, hp),
        in_specs=[qkv(0), qkv(1), qkv(2), tab, tab, tab, vec, vec,
                  pl.BlockSpec((LANES, LANES), lambda b, p: (0, 0))],
        out_specs=(out, out),
        scratch_shapes=[pltpu.VMEM((2, seq, LANES), BF16), pltpu.VMEM((seq, LANES), BF16),
                        pltpu.VMEM((2, seq, LANES), BF16)],
        compiler_params=_params("parallel", "parallel"), name=f"dilated_rate{rate}",
    )(proj3, proj3, proj3, *tables, gq, gk, bd)


def _ssd_kernel(z_ref, xs_ref, bc_ref, dt_ref, cwx_ref, cwb_ref, cbx_ref, cbb_ref, dtb_ref, alog_ref,
                dexp_ref, onorm_ref, tri_ref, triu_ref, exp_ref, o_ref, xpx_ref, xpb_ref, st_ref):
    L = SSM_CHUNK
    pad = 8

    @pl.when(pl.program_id(1) == 0)
    def _():
        xpx_ref[0:pad, :] = jnp.zeros((pad, SSM_INNER), F32)
        xpb_ref[0:pad, :] = jnp.zeros((pad, SSM_INNER), F32)
        st_ref[...] = jnp.zeros(st_ref.shape, F32)

    def conv_silu(src_ref, pad_ref, w_ref, b_ref):
        pad_ref[pad:, :] = src_ref[0]
        acc = b_ref[...] + w_ref[0:1, :] * pad_ref[pl.ds(pad - SSM_CONV + 1, L), :]
        for k in range(1, SSM_CONV):
            acc = acc + w_ref[k:k + 1, :] * pad_ref[pl.ds(pad - SSM_CONV + 1 + k, L), :]
        pad_ref[0:pad, :] = pad_ref[L:L + pad, :]
        return _silu(acc)

    xs = conv_silu(xs_ref, xpx_ref, cwx_ref, cbx_ref)
    bc = conv_silu(bc_ref, xpb_ref, cwb_ref, cbb_ref)
    gn = SSM_GROUPS * SSM_STATE
    bm, cm = bc[:, :gn], bc[:, gn:]

    xr = dt_ref[0] + dtb_ref[...]
    dt = jnp.maximum(xr, 0.0) + jnp.log(1.0 + jnp.exp(-jnp.abs(xr)))
    adt = dt * (-jnp.exp(alog_ref[...]))
    acs = _dot_exact_lhs(tri_ref[...], adt)
    acs_t = _dot_exact_rhs(adt.T, triu_ref[...])
    expand = exp_ref[...]
    dt_e = _dot_exact_rhs(dt, expand)
    acs_e = _dot_exact_rhs(acs, expand)
    xdt = xs * dt_e
    last = acs_e[L - 1:L, :]
    grow = jnp.exp(acs_e)
    to_end = jnp.exp(last - acs_e)
    chunk_decay = jnp.exp(last)

    ll = lax.broadcasted_iota(jnp.int32, (L, L), 0)
    ss = lax.broadcasted_iota(jnp.int32, (L, L), 1)
    causal = ll >= ss
    gw = SSM_INNER // SSM_GROUPS
    hpg = SSM_HEADS // SSM_GROUPS
    lane = lax.broadcasted_iota(jnp.int32, (L, gw), 1)
    nt = (((1,), (1,)), ((), ()))
    tn = (((0,), (0,)), ((), ()))
    ys = []
    for g in range(SSM_GROUPS):
        cols = slice(g * gw, (g + 1) * gw)
        bg = bm[:, g * SSM_STATE:(g + 1) * SSM_STATE].astype(BF16)
        cg = cm[:, g * SSM_STATE:(g + 1) * SSM_STATE].astype(BF16)
        xg = xdt[:, cols]
        cb = lax.dot_general(cg, bg, nt, preferred_element_type=F32)
        st = st_ref[g]
        y = jnp.dot(cg, st.astype(BF16), preferred_element_type=F32) * grow[:, cols]
        new = lax.dot_general(bg, (xg * to_end[:, cols]).astype(BF16), tn, preferred_element_type=F32)
        st_ref[g] = chunk_decay[:, cols] * st + new
        for hh in range(hpg):
            h = g * hpg + hh
            diff = acs[:, h:h + 1] - acs_t[h:h + 1, :]
            mat = (cb * jnp.exp(jnp.where(causal, diff, -jnp.inf))).astype(BF16)
            xh = jnp.where((lane >= hh * SSM_HEAD_DIM) & (lane < (hh + 1) * SSM_HEAD_DIM), xg, 0.0)
            y = y + jnp.dot(mat, xh.astype(BF16), preferred_element_type=F32)
        ys.append(y)
    y = jnp.concatenate(ys, axis=1) + xs * dexp_ref[...]
    yg = y * _silu(z_ref[0])
    ms = jnp.mean(yg * yg, axis=-1, keepdims=True)
    o_ref[0] = yg * lax.rsqrt(ms + NORM_EPS) * onorm_ref[...]


def ssd_mixer(proj3, dt3, conv_w, conv_b, dt_bias, a_log, d_exp, out_norm, consts):
    bsz, seq, _ = proj3.shape
    L = SSM_CHUNK
    w = SSM_INNER
    tri, triu, expand = consts
    col = lambda idx: pl.BlockSpec((1, L, w), lambda b, c, idx=idx: (b, c, idx))
    vecw = lambda rows, idx: pl.BlockSpec((rows, w), lambda b, c, idx=idx: (0, idx))
    vec = pl.BlockSpec((1, LANES), lambda b, c: (0, 0))
    sq = pl.BlockSpec((L, L), lambda b, c: (0, 0))
    return pl.pallas_call(
        _ssd_kernel, out_shape=jax.ShapeDtypeStruct((bsz, seq, w), F32), grid=(bsz, seq // L),
        in_specs=[col(COL_Z // w), col(COL_XBC // w), col(COL_XBC // w + 1),
                  pl.BlockSpec((1, L, LANES), lambda b, c: (b, c, 0)),
                  vecw(SSM_CONV, 0), vecw(SSM_CONV, 1), vecw(1, 0), vecw(1, 1), vec, vec,
                  vecw(1, 0), vecw(1, 0), sq, sq, pl.BlockSpec((LANES, w), lambda b, c: (0, 0))],
        out_specs=pl.BlockSpec((1, L, w), lambda b, c: (b, c, 0)),
        scratch_shapes=[pltpu.VMEM((L + 8, w), F32), pltpu.VMEM((L + 8, w), F32),
                        pltpu.VMEM((SSM_GROUPS, SSM_STATE, w // SSM_GROUPS), F32)],
        compiler_params=_params("parallel", "arbitrary"), name="ssd_mixer",
    )(proj3, proj3, proj3, dt3, conv_w, conv_w, conv_b, conv_b, dt_bias, a_log, d_exp, out_norm,
      tri, triu, expand)


def _merge_kernel(x_ref, a_ref, m_ref, o0_ref, o1_ref, o2_ref, l0_ref, l1_ref, l2_ref, gl_ref, bg_ref,
                  wa_ref, wm_ref, wc_ref, wo_ref, out_ref):
    l0, l1, l2 = l0_ref[...], l1_ref[...], l2_ref[...]
    lmax = jnp.maximum(jnp.maximum(l0, l1), l2)
    e0, e1, e2 = jnp.exp2(l0 - lmax), jnp.exp2(l1 - lmax), jnp.exp2(l2 - lmax)
    cmix = (e0 * o0_ref[...] + e1 * o1_ref[...] + e2 * o2_ref[...]) / (e0 + e1 + e2)
    gates = _sigmoid(gl_ref[...] + bg_ref[...])
    d = D_MODEL
    mm = lambda v, w_ref: jnp.dot(v.astype(BF16), w_ref[...], preferred_element_type=F32)
    merged = (gates[:, :d] * mm(a_ref[...], wa_ref) + gates[:, d:2 * d] * mm(m_ref[...], wm_ref)
              + gates[:, 2 * d:] * mm(cmix, wc_ref))
    out_ref[...] = x_ref[...] + mm(merged, wo_ref)


def merge_branches(x2, a2, m2, dil, proj2, b_gate, wa, wm, wc, wo, tm=256):
    n, d = x2.shape
    tm = min(tm, n)
    row = lambda width, idx=0: pl.BlockSpec((tm, width), lambda i, idx=idx: (i, idx))
    full = lambda arr: pl.BlockSpec(arr.shape, lambda i: (0, 0))
    (o0, l0), (o1, l1), (o2, l2) = dil
    gw = N_BRANCH * d
    return pl.pallas_call(
        _merge_kernel, out_shape=jax.ShapeDtypeStruct((n, d), F32), grid=(n // tm,),
        in_specs=[row(d), row(MOBA_WIDTH), row(SSM_INNER)] + [row(DIL_WIDTH)] * 6
                 + [row(gw, COL_DT // gw), full(b_gate), full(wa), full(wm), full(wc), full(wo)],
        out_specs=row(d), compiler_params=_params("parallel"), name="merge_branches",
    )(x2, a2, m2, o0, o1, o2, l0, l1, l2, proj2, b_gate, wa, wm, wc, wo)


def _ffn_kernel(x_ref, xh_ref, g_ref, wa_ref, wb_ref, cwa_ref, cwb_ref, cba_ref, cbb_ref, wd_ref, o_ref,
                u_ref, ua_ref, ub_ref, *, tm, tiles_per_seq):
    n = pl.program_id(1)

    def norm(x):
        ms = jnp.mean(x * x, axis=-1, keepdims=True)
        return (x * lax.rsqrt(ms + NORM_EPS) * g_ref[...]).astype(BF16)

    @pl.when(n == 0)
    def _():
        u_ref[0:HALO, :] = norm(xh_ref[...])
        u_ref[HALO:, :] = norm(x_ref[...])
        o_ref[...] = x_ref[...]

    keep = jnp.where(pl.program_id(0) % tiles_per_seq == 0, 0.0, 1.0)
    u = u_ref[...]

    def conv(w_ref, cw_ref, cb_ref, buf_ref):
        buf_ref[...] = jnp.dot(u, w_ref[...], preferred_element_type=F32)
        buf_ref[0:HALO, :] = buf_ref[0:HALO, :] * keep
        acc = cb_ref[...] + cw_ref[0:1, :] * buf_ref[pl.ds(HALO - FFN_CONV + 1, tm), :]
        for k in range(1, FFN_CONV):
            acc = acc + cw_ref[k:k + 1, :] * buf_ref[pl.ds(HALO - FFN_CONV + 1 + k, tm), :]
        return acc

    ga = conv(wa_ref, cwa_ref, cba_ref, ua_ref)
    gb = conv(wb_ref, cwb_ref, cbb_ref, ub_ref)
    h = (_silu(ga) * gb).astype(BF16)
    o_ref[...] += jnp.dot(h, wd_ref[...], preferred_element_type=F32)


def ffn(x2, gain, w_up, conv_w, conv_b, w_down, seq, tm=1024):
    n, d = x2.shape
    tm = min(tm, seq)
    ck = FFN_CHUNK
    nck = FFN_DIM // ck
    hb = tm // HALO
    return pl.pallas_call(
        functools.partial(_ffn_kernel, tm=tm, tiles_per_seq=seq // tm),
        out_shape=jax.ShapeDtypeStruct((n, d), F32), grid=(n // tm, nck),
        in_specs=[pl.BlockSpec((tm, d), lambda i, c: (i, 0)),
                  pl.BlockSpec((HALO, d), lambda i, c: (jnp.maximum(i * hb - 1, 0), 0)),
                  pl.BlockSpec((1, d), lambda i, c: (0, 0)),
                  pl.BlockSpec((d, ck), lambda i, c: (0, c)),
                  pl.BlockSpec((d, ck), lambda i, c: (0, nck + c)),
                  pl.BlockSpec((FFN_CONV, ck), lambda i, c: (0, c)),
                  pl.BlockSpec((FFN_CONV, ck), lambda i, c: (0, nck + c)),
                  pl.BlockSpec((1, ck), lambda i, c: (0, c)),
                  pl.BlockSpec((1, ck), lambda i, c: (0, nck + c)),
                  pl.BlockSpec((ck, d), lambda i, c: (c, 0))],
        out_specs=pl.BlockSpec((tm, d), lambda i, c: (i, 0)),
        scratch_shapes=[pltpu.VMEM((tm + HALO, d), BF16), pltpu.VMEM((tm + HALO, ck), F32),
                        pltpu.VMEM((tm + HALO, ck), F32)],
        compiler_params=_params("parallel", "arbitrary"), name="ffn",
    )(x2, x2, gain, w_up, w_up, conv_w, conv_w, conv_b, conv_b, w_down)


def _ple_kernel(x_ref, p_ref, g_ref, wg_ref, wp_ref, o_ref):
    x = x_ref[...]
    ms = jnp.mean(x * x, axis=-1, keepdims=True)
    u = (x * lax.rsqrt(ms + NORM_EPS) * g_ref[...]).astype(BF16)
    pg = _sigmoid(jnp.dot(u, wg_ref[...], preferred_element_type=F32))
    o_ref[...] = x + jnp.dot(p_ref[...].astype(BF16), wp_ref[...], preferred_element_type=F32) * pg


def ple(x2, p2, gain, w_gate, w_ple, tm=512):
    n, d = x2.shape
    tm = min(tm, n)
    return pl.pallas_call(
        _ple_kernel, out_shape=jax.ShapeDtypeStruct((n, d), F32), grid=(n // tm,),
        in_specs=[pl.BlockSpec((tm, d), lambda i: (i, 0)), pl.BlockSpec((tm, PLE_DIM), lambda i: (i, 0)),
                  pl.BlockSpec((1, d), lambda i: (0, 0)), pl.BlockSpec((d, d), lambda i: (0, 0)),
                  pl.BlockSpec((PLE_DIM, d), lambda i: (0, 0))],
        out_specs=pl.BlockSpec((tm, d), lambda i: (i, 0)),
        compiler_params=_params("parallel"), name="ple")(x2, p2, gain, w_gate, w_ple)


def _constants():
    lane = np.arange(LANES)
    bd = (lane[:, None] // HEAD_DIM == lane[None, :] // HEAD_DIM).astype(np.float32) / HEAD_DIM
    r = np.arange(SSM_CHUNK)
    tri = (r[None, :] <= r[:, None]).astype(np.float32)
    expand = (lane[:, None] == (np.arange(SSM_INNER)[None, :] // SSM_HEAD_DIM)).astype(np.float32)
    as_bf16 = lambda a: jnp.asarray(a, dtype=BF16)
    return as_bf16(bd), (as_bf16(tri), as_bf16(tri.T), as_bf16(expand))


def _pad_lanes(v):
    return jnp.pad(v, (0, LANES - v.shape[0]))[None, :]


def kernel(x, p, positions, norm_mix, w_in, b_gate, moba_q_norm, moba_k_norm, dil_q_norm, dil_k_norm,
           ssm_conv_w, ssm_conv_b, ssm_dt_bias, ssm_a_log, ssm_d, ssm_out_norm, w_br_moba, w_br_ssm,
           w_br_dil, w_out, norm_ffn, w_up, ffn_conv_w, ffn_conv_b, w_down, norm_ple, w_ple_gate, w_ple):
    bsz, seq, d = x.shape
    depth = w_in.shape[0]
    n = bsz * seq
    bd, ssd_consts = _constants()
    tables = rope_tables(positions)
    row = lambda v: v[None, :]
    two = lambda v: jnp.tile(v, 2)[None, :]

    x2 = x.reshape(n, d)
    for i in range(depth):
        w_main = jnp.concatenate([w_in[i][:, :COL_DT], w_in[i][:, COL_GATE:]], axis=1).astype(BF16)
        w_dt = jnp.pad(w_in[i][:, COL_DT:COL_GATE], ((0, 0), (0, LANES - SSM_HEADS))).astype(BF16)
        proj2, dt2 = in_projection(x2, row(norm_mix[i]), w_main, w_dt)
        proj3 = proj2.reshape(bsz, seq, PROJ_COLS)

        out_a = moba_attention(proj3, tables, two(moba_q_norm[i]), two(moba_k_norm[i]), bd)
        out_b = ssd_mixer(proj3, dt2.reshape(bsz, seq, LANES), ssm_conv_w[i], row(ssm_conv_b[i]),
                          _pad_lanes(ssm_dt_bias[i]), _pad_lanes(ssm_a_log[i]),
                          row(jnp.repeat(ssm_d[i], SSM_HEAD_DIM)), row(ssm_out_norm[i]), ssd_consts)
        dil = [dilated_group(proj3, tables, two(dil_q_norm[i]), two(dil_k_norm[i]), bd, g)
               for g in range(DIL_GROUPS)]
        dil2 = [(o.reshape(n, DIL_WIDTH), l.reshape(n, DIL_WIDTH)) for o, l in dil]

        x2 = merge_branches(x2, out_a.reshape(n, MOBA_WIDTH), out_b.reshape(n, SSM_INNER), dil2, proj2,
                            row(b_gate[i]), w_br_moba[i].astype(BF16), w_br_ssm[i].astype(BF16),
                            w_br_dil[i].astype(BF16), w_out[i].astype(BF16))
        x2 = ffn(x2, row(norm_ffn[i]), w_up[i].astype(BF16), ffn_conv_w[i], row(ffn_conv_b[i]),
                 w_down[i].astype(BF16), seq)
        x2 = ple(x2, p[i].reshape(n, PLE_DIM), row(norm_ple[i]), w_ple_gate[i].astype(BF16),
                 w_ple[i].astype(BF16))
    return x2.reshape(bsz, seq, d)
```

```python
import functools
import math

import numpy as np
import jax
import jax.numpy as jnp
from jax import lax
from jax.experimental import pallas as pl
from jax.experimental.pallas import tpu as pltpu

F32 = jnp.float32
BF16 = jnp.bfloat16

D_MODEL = 1024
PLE_DIM = 256
HEAD_DIM = 64
ROPE_DIM = HEAD_DIM // 4
ROPE_THETA = 500000.0
NORM_EPS = 1e-6
NEG_INF = -1e30

MOBA_HEADS = 8
MOBA_BLOCK = 256
MOBA_TOPK = 3
MOBA_WIDTH = MOBA_HEADS * HEAD_DIM

DIL_RATES = (1, 4, 16)
DIL_GROUPS = 3
DIL_HEADS = 8
DIL_WINDOW = 128
DIL_WIDTH = DIL_HEADS * HEAD_DIM
DIL_QBLOCK = 128
DIL_UNROLL = 4

SSM_INNER = D_MODEL
SSM_HEAD_DIM = 64
SSM_HEADS = SSM_INNER // SSM_HEAD_DIM
SSM_GROUPS = 4
SSM_STATE = 128
SSM_CONV = 4
SSM_CHUNK = 128
SSM_XBC = SSM_INNER + 2 * SSM_GROUPS * SSM_STATE

FFN_DIM = 2816
FFN_CONV = 3
FFN_CHUNK = 256
N_BRANCH = 3

COL_MOBA = 0
COL_DIL = 3 * MOBA_WIDTH
COL_Z = COL_DIL + 3 * DIL_GROUPS * DIL_WIDTH
COL_XBC = COL_Z + SSM_INNER
COL_DT = COL_XBC + SSM_XBC
COL_GATE = COL_DT + SSM_HEADS
IN_COLS = COL_GATE + N_BRANCH * D_MODEL
PROJ_COLS = COL_DT + N_BRANCH * D_MODEL

QSCALE = HEAD_DIM ** -0.5 * math.log2(math.e)

LANES = 128
HALO = 16
VMEM_LIMIT = 56 * 1024 * 1024


def _params(*sem):
    return pltpu.CompilerParams(dimension_semantics=sem, vmem_limit_bytes=VMEM_LIMIT)


def _silu(x):
    return x * (1.0 / (1.0 + jnp.exp(-x)))


def _sigmoid(x):
    return 1.0 / (1.0 + jnp.exp(-x))


def _split3(a):
    a1 = a.astype(BF16)
    r1 = a - a1.astype(F32)
    a2 = r1.astype(BF16)
    a3 = (r1 - a2.astype(F32)).astype(BF16)
    return a1, a2, a3


def _dot_exact_rhs(a, b_exact, passes=3):
    out = None
    for piece in _split3(a)[:passes]:
        t = jnp.dot(piece, b_exact, preferred_element_type=F32)
        out = t if out is None else out + t
    return out


def _dot_exact_lhs(a_exact, b, passes=3):
    out = None
    for piece in _split3(b)[:passes]:
        t = jnp.dot(a_exact, piece, preferred_element_type=F32)
        out = t if out is None else out + t
    return out


def _head_norm_rope(x, gain, bd, cos, sin_up, sin_dn):
    ms = _dot_exact_rhs(x * x, bd, passes=2)
    y = x * lax.rsqrt(ms + NORM_EPS) * gain
    half = ROPE_DIM // 2
    return y * cos + pltpu.roll(y, half, 1) * sin_up + pltpu.roll(y, LANES - half, 1) * sin_dn


def _rope_kernel(pos_ref, inv_ref, cos_ref, up_ref, dn_ref):
    ang = pos_ref[0] * inv_ref[...]
    d = lax.broadcasted_iota(jnp.int32, ang.shape, 1) % HEAD_DIM
    half = ROPE_DIM // 2
    s = jnp.sin(ang)
    cos_ref[0] = jnp.cos(ang)
    up_ref[0] = jnp.where((d >= half) & (d < ROPE_DIM), s, 0.0)
    dn_ref[0] = jnp.where(d < half, -s, 0.0)


def rope_tables(positions):
    bsz, seq = positions.shape
    ts = min(seq, 1024)
    d = np.arange(LANES) % HEAD_DIM
    inv = ROPE_THETA ** (-jnp.arange(0, ROPE_DIM, 2, dtype=F32) / ROPE_DIM)
    inv_lane = jnp.where(d < ROPE_DIM, inv[d % (ROPE_DIM // 2)], 0.0).astype(F32)[None, :]
    pos = positions.astype(F32)[..., None]
    shp = jax.ShapeDtypeStruct((bsz, seq, LANES), F32)
    spec = pl.BlockSpec((1, ts, LANES), lambda b, t: (b, t, 0))
    return pl.pallas_call(
        _rope_kernel, out_shape=(shp, shp, shp), grid=(bsz, seq // ts),
        in_specs=[pl.BlockSpec((1, ts, 1), lambda b, t: (b, t, 0)),
                  pl.BlockSpec((1, LANES), lambda b, t: (0, 0))],
        out_specs=(spec, spec, spec), compiler_params=_params("parallel", "parallel"),
        name="rope_tables")(pos, inv_lane)


def _inproj_kernel(x_ref, g_ref, w_ref, wdt_ref, o_ref, dt_ref, u_ref):
    @pl.when(pl.program_id(1) == 0)
    def _():
        x = x_ref[...]
        ms = jnp.mean(x * x, axis=-1, keepdims=True)
        u = (x * lax.rsqrt(ms + NORM_EPS) * g_ref[...]).astype(BF16)
        u_ref[...] = u
        dt_ref[...] = jnp.dot(u, wdt_ref[...], preferred_element_type=F32)

    o_ref[...] = jnp.dot(u_ref[...], w_ref[...], preferred_element_type=F32)


def in_projection(x2, gain, w_main, w_dt, tm=1024, tn=1024):
    n, d = x2.shape
    cols = w_main.shape[1]
    tm = min(tm, n)
    return pl.pallas_call(
        _inproj_kernel,
        out_shape=(jax.ShapeDtypeStruct((n, cols), F32), jax.ShapeDtypeStruct((n, LANES), F32)),
        grid=(n // tm, cols // tn),
        in_specs=[pl.BlockSpec((tm, d), lambda i, j: (i, 0)),
                  pl.BlockSpec((1, d), lambda i, j: (0, 0)),
                  pl.BlockSpec((d, tn), lambda i, j: (0, j)),
                  pl.BlockSpec((d, LANES), lambda i, j: (0, 0))],
        out_specs=(pl.BlockSpec((tm, tn), lambda i, j: (i, j)),
                   pl.BlockSpec((tm, LANES), lambda i, j: (i, 0))),
        scratch_shapes=[pltpu.VMEM((tm, d), BF16)],
        compiler_params=_params("parallel", "arbitrary"), name="in_projection")(x2, gain, w_main, w_dt)


def _head_lanes(shape):
    lane = lax.broadcasted_iota(jnp.int32, shape, len(shape) - 1)
    return lane, (lane < HEAD_DIM, lane >= HEAD_DIM), (HEAD_DIM, 0)


def _moba_kernel(q_ref, k_ref, v_ref, cos_ref, up_ref, dn_ref, gq_ref, gk_ref, bd_ref, o_ref,
                 qa_ref, ka_ref, va_ref, qf_ref, km_ref, *, nb):
    blk = MOBA_BLOCK
    bd = bd_ref[...]
    lane, head, aux = _head_lanes((blk, LANES))
    nbp = km_ref.shape[0]
    km_ref[...] = jnp.zeros(km_ref.shape, F32)

    def prep(j):
        rows = pl.ds(pl.multiple_of(j * blk, blk), blk)
        cos, up, dn = cos_ref[0, rows, :], up_ref[0, rows, :], dn_ref[0, rows, :]
        qf_ref[rows, :] = _head_norm_rope(q_ref[0, rows, :], gq_ref[...], bd, cos, up, dn)
        kn = _head_norm_rope(k_ref[0, rows, :], gk_ref[...], bd, cos, up, dn)
        v = v_ref[0, rows, :]
        km_ref[pl.ds(j, 1), :] = jnp.mean(kn, axis=0, keepdims=True)
        for h in range(2):
            ka_ref[h, rows, :] = jnp.where(head[h], kn, jnp.where(lane == aux[h] + j, 1.0, 0.0)).astype(BF16)
            va_ref[h, rows, :] = jnp.where(head[h], v, jnp.where(lane == aux[h], 1.0, 0.0)).astype(BF16)

    def prep2(t, carry):
        prep(2 * t)
        prep(2 * t + 1)
        return carry

    lax.fori_loop(0, nb // 2, prep2, 0)

    bidx = lax.broadcasted_iota(jnp.int32, (nbp, blk), 0).astype(F32)
    isblk = bidx < float(nb)
    _, head_k, _ = _head_lanes((nbp, LANES))
    nt = (((1,), (1,)), ((), ()))

    def select(i):
        rows = pl.ds(pl.multiple_of(i * blk, blk), blk)
        qf = qf_ref[rows, :]
        q_hi = qf.astype(BF16)
        q_lo = (qf - q_hi.astype(F32)).astype(BF16)
        i_f = lax.convert_element_type(i, F32)
        for h in range(2):
            km = jnp.where(head_k[h], km_ref[...], 0.0)
            k_hi = km.astype(BF16)
            k_lo = (km - k_hi.astype(F32)).astype(BF16)
            dot = lambda a, b: lax.dot_general(a, b, nt, preferred_element_type=F32)
            sc = dot(k_hi, q_hi) + (dot(k_hi, q_lo) + dot(k_lo, q_hi))
            valid = isblk & (bidx < i_f)
            cur = jnp.where(valid, sc, -jnp.inf)
            sel = jnp.zeros((nbp, blk), F32)
            for _ in range(min(MOBA_TOPK, nb)):
                mx = jnp.max(cur, axis=0, keepdims=True)
                first = jnp.min(jnp.where((cur == mx) & isblk, bidx, float(nbp)), axis=0, keepdims=True)
                hit = bidx == first
                sel = jnp.where(hit, 1.0, sel)
                cur = jnp.where(hit, -jnp.inf, cur)
            keep = ((sel > 0.5) & valid) | (bidx == i_f)
            bias = jnp.where(isblk & jnp.logical_not(keep), NEG_INF, 0.0)
            pieces = [jnp.zeros((aux[h], blk), F32)] if aux[h] else []
            pieces += [bias, jnp.zeros((LANES - aux[h] - nbp, blk), F32)]
            bias_t = jnp.concatenate(pieces, axis=0).T
            qa_ref[h, rows, :] = jnp.where(head[h], qf * QSCALE, bias_t).astype(BF16)

    def select2(t, carry):
        select(2 * t)
        select(2 * t + 1)
        return carry

    lax.fori_loop(0, nb // 2, select2, 0)

    wide = 2 * blk
    _, head_w, _ = _head_lanes((wide, LANES))
    causal = (lax.broadcasted_iota(jnp.int32, (wide, wide), 1)
              <= lax.broadcasted_iota(jnp.int32, (wide, wide), 0))
    nt = (((1,), (1,)), ((), ()))

    def rows_of(t):
        return slice(t * wide, (t + 1) * wide)

    def logits(a, g):
        return [lax.dot_general(qa_ref[h, rows_of(a), :], ka_ref[h, rows_of(g), :], nt,
                                preferred_element_type=F32) for h in range(2)]

    tiles = [(a, g) for a in range(nb // 2) for g in [a] + list(range(a))]
    ss = logits(*tiles[0])
    state = None
    for t, (a, g) in enumerate(tiles):
        nxt = logits(*tiles[t + 1]) if t + 1 < len(tiles) else None
        new = []
        for h in range(2):
            s = ss[h]
            if g == a:
                s = jnp.where(causal, s, NEG_INF)
                m = jnp.max(s, axis=-1, keepdims=True)
                acc = jnp.dot(jnp.exp2(s - m).astype(BF16), va_ref[h, rows_of(g), :],
                              preferred_element_type=F32)
            else:
                m_old, acc_old = state[h]
                m = jnp.maximum(m_old, jnp.max(s, axis=-1, keepdims=True))
                acc = jnp.exp2(m_old - m) * acc_old + jnp.dot(
                    jnp.exp2(s - m).astype(BF16), va_ref[h, rows_of(g), :], preferred_element_type=F32)
            new.append((m, acc))
        state, ss = new, nxt
        if g == a - 1 or a == 0:
            den0 = state[0][1][:, aux[0]:aux[0] + 1]
            den1 = state[1][1][:, aux[1]:aux[1] + 1]
            o_ref[0, rows_of(a), :] = jnp.where(head_w[0], state[0][1] / den0, state[1][1] / den1)


def moba_attention(proj3, tables, gq, gk, bd):
    bsz, seq, _ = proj3.shape
    nb = seq // MOBA_BLOCK
    assert nb % 2 == 0 and nb <= HEAD_DIM, "key blocks are visited in pairs and indexed on 64 spare lanes"
    hp = MOBA_WIDTH // LANES
    cos, up, dn = tables
    qkv = lambda part: pl.BlockSpec((1, seq, LANES), lambda b, p, part=part: (b, 0, part * hp + p))
    tab = pl.BlockSpec((1, seq, LANES), lambda b, p: (b, 0, 0))
    vec = pl.BlockSpec((1, LANES), lambda b, p: (0, 0))
    return pl.pallas_call(
        functools.partial(_moba_kernel, nb=nb),
        out_shape=jax.ShapeDtypeStruct((bsz, seq, MOBA_WIDTH), F32), grid=(bsz, hp),
        in_specs=[qkv(0), qkv(1), qkv(2), tab, tab, tab, vec, vec,
                  pl.BlockSpec((LANES, LANES), lambda b, p: (0, 0))],
        out_specs=pl.BlockSpec((1, seq, LANES), lambda b, p: (b, 0, p)),
        scratch_shapes=[pltpu.VMEM((2, seq, LANES), BF16), pltpu.VMEM((2, seq, LANES), BF16),
                        pltpu.VMEM((2, seq, LANES), BF16), pltpu.VMEM((seq, LANES), F32),
                        pltpu.VMEM((-(-nb // 8) * 8, LANES), F32)],
        compiler_params=_params("parallel", "parallel"), name="moba_attention",
    )(proj3, proj3, proj3, cos, up, dn, gq, gk, bd)


def _dilated_kernel(q_ref, k_ref, v_ref, cos_ref, up_ref, dn_ref, gq_ref, gk_ref, bd_ref,
                    o_ref, lse_ref, qd_ref, kd_ref, va_ref, *, seq, rate):
    qb = DIL_QBLOCK
    cpb = seq // rate // qb
    shift = cpb.bit_length() - 1
    bd = bd_ref[...]
    lane, head, aux = _head_lanes((qb, LANES))

    def token_rows(n):
        if rate == 1:
            return pl.ds(pl.multiple_of(n * qb, qb), qb)
        c, ch = lax.shift_right_logical(n, shift), n & (cpb - 1)
        return pl.ds(c + ch * (qb * rate), qb, stride=rate)

    def prep(n):
        src = token_rows(n)
        dst = pl.ds(pl.multiple_of(n * qb, qb), qb)
        cos, up, dn = cos_ref[0, src, :], up_ref[0, src, :], dn_ref[0, src, :]
        qn = _head_norm_rope(q_ref[0, src, :], gq_ref[...], bd, cos, up, dn)
        kn = _head_norm_rope(k_ref[0, src, :], gk_ref[...], bd, cos, up, dn)
        v = v_ref[0, src, :]
        qd_ref[dst, :] = (qn * QSCALE).astype(BF16)
        for h in range(2):
            kd_ref[h, dst, :] = jnp.where(head[h], kn, 0.0).astype(BF16)
            va_ref[h, dst, :] = jnp.concatenate(
                [jnp.where(head[h], v, 0.0), jnp.where(head[h], 1.0, 0.0)], axis=1).astype(BF16)

    def unrolled(fn):
        def step(it, carry):
            for u in range(DIL_UNROLL):
                fn(it * DIL_UNROLL + u)
            return carry
        lax.fori_loop(0, seq // qb // DIL_UNROLL, step, 0)

    unrolled(prep)

    rel = (lax.broadcasted_iota(jnp.int32, (qb, 2 * qb), 0)
           - lax.broadcasted_iota(jnp.int32, (qb, 2 * qb), 1))
    nt = (((1,), (1,)), ((), ()))

    def qblock(n):
        first = (n & (cpb - 1)) == 0
        k0 = jnp.where(first, n, n - 1) * qb
        krows = pl.ds(pl.multiple_of(k0, qb), 2 * qb)
        qrows = pl.ds(pl.multiple_of(n * qb, qb), qb)
        dist = rel + (n * qb - k0)
        ok = (dist >= 0) & (dist <= DIL_WINDOW)
        keys = jnp.concatenate([kd_ref[0, krows, :], kd_ref[1, krows, :]], axis=0)
        s = lax.dot_general(qd_ref[qrows, :], keys, nt, preferred_element_type=F32)
        ms, ps = [], []
        for h in range(2):
            sh = jnp.where(ok, s[:, h * 2 * qb:(h + 1) * 2 * qb], NEG_INF)
            ms.append(jnp.max(sh, axis=-1, keepdims=True))
            ps.append(jnp.exp2(sh - ms[h]).astype(BF16))
        vals = jnp.concatenate([va_ref[0, krows, :], va_ref[1, krows, :]], axis=0)
        acc = jnp.dot(jnp.concatenate(ps, axis=1), vals, preferred_element_type=F32)
        den = acc[:, LANES:]
        dst = token_rows(n)
        o_ref[0, dst, :] = acc[:, :LANES] / den
        lse_ref[0, dst, :] = jnp.where(head[0], ms[0], ms[1]) + jnp.log2(den)

    unrolled(qblock)


def dilated_group(proj3, tables, gq, gk, bd, group):
    bsz, seq, cols = proj3.shape
    rate = DIL_RATES[group]
    cpb = seq // rate // DIL_QBLOCK
    assert cpb >= 2 and cpb & (cpb - 1) == 0 and (seq // DIL_QBLOCK) % DIL_UNROLL == 0
    hp = DIL_WIDTH // LANES
    base = COL_DIL // LANES
    qkv = lambda part: pl.BlockSpec(
        (1, seq, LANES), lambda b, p, part=part: (b, 0, base + (part * DIL_GROUPS + group) * hp + p))
    tab = pl.BlockSpec((1, seq, LANES), lambda b, p: (b, 0, 0))
    vec = pl.BlockSpec((1, LANES), lambda b, p: (0, 0))
    out = pl.BlockSpec((1, seq, LANES), lambda b, p: (b, 0, p))
    shp = jax.ShapeDtypeStruct((bsz, seq, DIL_WIDTH), F32)
    return pl.pallas_call(
        functools.partial(_dilated_kernel, seq=seq, rate=rate), out_shape=(shp, shp), grid=(bsz, hp),
        in_specs=[qkv(0), qkv(1), qkv(2), tab, tab, tab, vec, vec,
                  pl.BlockSpec((LANES, LANES), lambda b, p: (0, 0))],
        out_specs=(out, out),
        scratch_shapes=[pltpu.VMEM((seq, LANES), BF16), pltpu.VMEM((2, seq, LANES), BF16),
                        pltpu.VMEM((2, seq, 2 * LANES), BF16)],
        compiler_params=_params("parallel", "parallel"), name=f"dilated_rate{rate}",
    )(proj3, proj3, proj3, *tables, gq, gk, bd)


def _ssd_kernel(z_ref, xs_ref, bc_ref, dt_ref, cwx_ref, cwb_ref, cbx_ref, cbb_ref, dtb_ref, alog_ref,
                dexp_ref, onorm_ref, tri_ref, triu_ref, exp_ref, o_ref, xpx_ref, xpb_ref, st_ref):
    L = SSM_CHUNK
    pad = 8

    @pl.when(pl.program_id(1) == 0)
    def _():
        xpx_ref[0:pad, :] = jnp.zeros((pad, SSM_INNER), F32)
        xpb_ref[0:pad, :] = jnp.zeros((pad, SSM_INNER), F32)
        st_ref[...] = jnp.zeros(st_ref.shape, F32)

    def conv_silu(src_ref, pad_ref, w_ref, b_ref):
        pad_ref[pad:, :] = src_ref[0]
        acc = b_ref[...] + w_ref[0:1, :] * pad_ref[pl.ds(pad - SSM_CONV + 1, L), :]
        for k in range(1, SSM_CONV):
            acc = acc + w_ref[k:k + 1, :] * pad_ref[pl.ds(pad - SSM_CONV + 1 + k, L), :]
        pad_ref[0:pad, :] = pad_ref[L:L + pad, :]
        return _silu(acc)

    xs = conv_silu(xs_ref, xpx_ref, cwx_ref, cbx_ref)
    bc = conv_silu(bc_ref, xpb_ref, cwb_ref, cbb_ref)
    gn = SSM_GROUPS * SSM_STATE
    bm, cm = bc[:, :gn], bc[:, gn:]

    xr = dt_ref[0] + dtb_ref[...]
    dt = jnp.maximum(xr, 0.0) + jnp.log(1.0 + jnp.exp(-jnp.abs(xr)))
    adt = dt * (-jnp.exp(alog_ref[...]))
    acs = _dot_exact_lhs(tri_ref[...], adt)
    acs_t = _dot_exact_rhs(adt.T, triu_ref[...])
    expand = exp_ref[...]
    dt_e = _dot_exact_rhs(dt, expand)
    acs_e = _dot_exact_rhs(acs, expand)
    xdt = xs * dt_e
    last = acs_e[L - 1:L, :]
    grow = jnp.exp(acs_e)
    to_end = jnp.exp(last - acs_e)
    chunk_decay = jnp.exp(last)

    ll = lax.broadcasted_iota(jnp.int32, (L, L), 0)
    ss = lax.broadcasted_iota(jnp.int32, (L, L), 1)
    causal = ll >= ss
    gw = SSM_INNER // SSM_GROUPS
    hpg = SSM_HEADS // SSM_GROUPS
    lane = lax.broadcasted_iota(jnp.int32, (L, gw), 1)
    nt = (((1,), (1,)), ((), ()))
    tn = (((0,), (0,)), ((), ()))
    ys = []
    for g in range(SSM_GROUPS):
        cols = slice(g * gw, (g + 1) * gw)
        bg = bm[:, g * SSM_STATE:(g + 1) * SSM_STATE].astype(BF16)
        cg = cm[:, g * SSM_STATE:(g + 1) * SSM_STATE].astype(BF16)
        xg = xdt[:, cols]
        cb = lax.dot_general(cg, bg, nt, preferred_element_type=F32)
        st = st_ref[g]
        y = jnp.dot(cg, st.astype(BF16), preferred_element_type=F32) * grow[:, cols]
        new = lax.dot_general(bg, (xg * to_end[:, cols]).astype(BF16), tn, preferred_element_type=F32)
        st_ref[g] = chunk_decay[:, cols] * st + new
        for hh in range(hpg):
            h = g * hpg + hh
            diff = acs[:, h:h + 1] - acs_t[h:h + 1, :]
            mat = (cb * jnp.exp(jnp.where(causal, diff, -jnp.inf))).astype(BF16)
            xh = jnp.where((lane >= hh * SSM_HEAD_DIM) & (lane < (hh + 1) * SSM_HEAD_DIM), xg, 0.0)
            y = y + jnp.dot(mat, xh.astype(BF16), preferred_element_type=F32)
        ys.append(y)
    y = jnp.concatenate(ys, axis=1) + xs * dexp_ref[...]
    yg = y * _silu(z_ref[0])
    ms = jnp.mean(yg * yg, axis=-1, keepdims=True)
    o_ref[0] = yg * lax.rsqrt(ms + NORM_EPS) * onorm_ref[...]


def ssd_mixer(proj3, dt3, conv_w, conv_b, dt_bias, a_log, d_exp, out_norm, consts):
    bsz, seq, _ = proj3.shape
    L = SSM_CHUNK
    w = SSM_INNER
    tri, triu, expand = consts
    col = lambda idx: pl.BlockSpec((1, L, w), lambda b, c, idx=idx: (b, c, idx))
    vecw = lambda rows, idx: pl.BlockSpec((rows, w), lambda b, c, idx=idx: (0, idx))
    vec = pl.BlockSpec((1, LANES), lambda b, c: (0, 0))
    sq = pl.BlockSpec((L, L), lambda b, c: (0, 0))
    return pl.pallas_call(
        _ssd_kernel, out_shape=jax.ShapeDtypeStruct((bsz, seq, w), F32), grid=(bsz, seq // L),
        in_specs=[col(COL_Z // w), col(COL_XBC // w), col(COL_XBC // w + 1),
                  pl.BlockSpec((1, L, LANES), lambda b, c: (b, c, 0)),
                  vecw(SSM_CONV, 0), vecw(SSM_CONV, 1), vecw(1, 0), vecw(1, 1), vec, vec,
                  vecw(1, 0), vecw(1, 0), sq, sq, pl.BlockSpec((LANES, w), lambda b, c: (0, 0))],
        out_specs=pl.BlockSpec((1, L, w), lambda b, c: (b, c, 0)),
        scratch_shapes=[pltpu.VMEM((L + 8, w), F32), pltpu.VMEM((L + 8, w), F32),
                        pltpu.VMEM((SSM_GROUPS, SSM_STATE, w // SSM_GROUPS), F32)],
        compiler_params=_params("parallel", "arbitrary"), name="ssd_mixer",
    )(proj3, proj3, proj3, dt3, conv_w, conv_w, conv_b, conv_b, dt_bias, a_log, d_exp, out_norm,
      tri, triu, expand)


def _merge_kernel(x_ref, a_ref, m_ref, o0_ref, o1_ref, o2_ref, l0_ref, l1_ref, l2_ref, gl_ref, bg_ref,
                  wa_ref, wm_ref, wc_ref, wo_ref, out_ref):
    l0, l1, l2 = l0_ref[...], l1_ref[...], l2_ref[...]
    lmax = jnp.maximum(jnp.maximum(l0, l1), l2)
    e0, e1, e2 = jnp.exp2(l0 - lmax), jnp.exp2(l1 - lmax), jnp.exp2(l2 - lmax)
    cmix = (e0 * o0_ref[...] + e1 * o1_ref[...] + e2 * o2_ref[...]) / (e0 + e1 + e2)
    gates = _sigmoid(gl_ref[...] + bg_ref[...])
    d = D_MODEL
    mm = lambda v, w_ref: jnp.dot(v.astype(BF16), w_ref[...], preferred_element_type=F32)
    merged = (gates[:, :d] * mm(a_ref[...], wa_ref) + gates[:, d:2 * d] * mm(m_ref[...], wm_ref)
              + gates[:, 2 * d:] * mm(cmix, wc_ref))
    out_ref[...] = x_ref[...] + mm(merged, wo_ref)


def merge_branches(x2, a2, m2, dil, proj2, b_gate, wa, wm, wc, wo, tm=256):
    n, d = x2.shape
    tm = min(tm, n)
    row = lambda width, idx=0: pl.BlockSpec((tm, width), lambda i, idx=idx: (i, idx))
    full = lambda arr: pl.BlockSpec(arr.shape, lambda i: (0, 0))
    (o0, l0), (o1, l1), (o2, l2) = dil
    gw = N_BRANCH * d
    return pl.pallas_call(
        _merge_kernel, out_shape=jax.ShapeDtypeStruct((n, d), F32), grid=(n // tm,),
        in_specs=[row(d), row(MOBA_WIDTH), row(SSM_INNER)] + [row(DIL_WIDTH)] * 6
                 + [row(gw, COL_DT // gw), full(b_gate), full(wa), full(wm), full(wc), full(wo)],
        out_specs=row(d), compiler_params=_params("parallel"), name="merge_branches",
    )(x2, a2, m2, o0, o1, o2, l0, l1, l2, proj2, b_gate, wa, wm, wc, wo)


def _rms_bf16(x, gain):
    ms = jnp.mean(x * x, axis=-1, keepdims=True)
    return (x * lax.rsqrt(ms + NORM_EPS) * gain).astype(BF16)


def _ffn_ple_kernel(x_ref, xh_ref, p_ref, gf_ref, wup_ref, cw_ref, cb_ref, wd_ref, gp_ref, wg_ref, wp_ref,
                    o_ref, u_ref, buf_ref, *, tm, tiles_per_seq):
    ck = FFN_CHUNK
    nck = FFN_DIM // ck
    u_ref[0:HALO, :] = _rms_bf16(xh_ref[...], gf_ref[...])
    u_ref[HALO:, :] = _rms_bf16(x_ref[...], gf_ref[...])
    keep = jnp.where(pl.program_id(0) % tiles_per_seq == 0, 0.0, 1.0)

    def up(c):
        u = u_ref[...]
        return [jnp.dot(u, wup_ref[:, half * FFN_DIM + c * ck:half * FFN_DIM + (c + 1) * ck],
                        preferred_element_type=F32) for half in range(2)]

    def conv(c, half, val):
        buf = buf_ref.at[2 * (c % 2) + half]
        cols = slice(half * FFN_DIM + c * ck, half * FFN_DIM + (c + 1) * ck)
        buf[0:HALO, :] = val[0:HALO, :] * keep
        buf[HALO:, :] = val[HALO:, :]
        acc = cb_ref[:, cols] + cw_ref[0:1, cols] * buf[pl.ds(HALO - FFN_CONV + 1, tm), :]
        for k in range(1, FFN_CONV):
            acc = acc + cw_ref[k:k + 1, cols] * buf[pl.ds(HALO - FFN_CONV + 1 + k, tm), :]
        return acc

    cur = up(0)
    acc = x_ref[...]
    for c in range(nck):
        nxt = up(c + 1) if c + 1 < nck else None
        h = (_silu(conv(c, 0, cur[0])) * conv(c, 1, cur[1])).astype(BF16)
        acc = acc + jnp.dot(h, wd_ref[c * ck:(c + 1) * ck, :], preferred_element_type=F32)
        cur = nxt
    pg = _sigmoid(jnp.dot(_rms_bf16(acc, gp_ref[...]), wg_ref[...], preferred_element_type=F32))
    o_ref[...] = acc + jnp.dot(p_ref[...].astype(BF16), wp_ref[...], preferred_element_type=F32) * pg


def ffn_ple(x2, p2, gain_ffn, w_up, conv_w, conv_b, w_down, gain_ple, w_gate, w_ple, seq, tm=512):
    n, d = x2.shape
    tm = min(tm, seq)
    hb = tm // HALO
    row = lambda width: pl.BlockSpec((tm, width), lambda i: (i, 0))
    resident = lambda arr: pl.BlockSpec(arr.shape, lambda i: (0, 0), pipeline_mode=pl.Buffered(1))
    return pl.pallas_call(
        functools.partial(_ffn_ple_kernel, tm=tm, tiles_per_seq=seq // tm),
        out_shape=jax.ShapeDtypeStruct((n, d), F32), grid=(n // tm,),
        in_specs=[row(d), pl.BlockSpec((HALO, d), lambda i: (jnp.maximum(i * hb - 1, 0), 0)), row(PLE_DIM),
                  resident(gain_ffn), resident(w_up), resident(conv_w), resident(conv_b), resident(w_down),
                  resident(gain_ple), resident(w_gate), resident(w_ple)],
        out_specs=row(d),
        scratch_shapes=[pltpu.VMEM((tm + HALO, d), BF16), pltpu.VMEM((4, tm + HALO, FFN_CHUNK), F32)],
        compiler_params=_params("parallel"), name="ffn_ple",
    )(x2, x2, p2, gain_ffn, w_up, conv_w, conv_b, w_down, gain_ple, w_gate, w_ple)


def _constants():
    lane = np.arange(LANES)
    bd = (lane[:, None] // HEAD_DIM == lane[None, :] // HEAD_DIM).astype(np.float32) / HEAD_DIM
    r = np.arange(SSM_CHUNK)
    tri = (r[None, :] <= r[:, None]).astype(np.float32)
    expand = (lane[:, None] == (np.arange(SSM_INNER)[None, :] // SSM_HEAD_DIM)).astype(np.float32)
    as_bf16 = lambda a: jnp.asarray(a, dtype=BF16)
    return as_bf16(bd), (as_bf16(tri), as_bf16(tri.T), as_bf16(expand))


def _pad_lanes(v):
    return jnp.pad(v, (0, LANES - v.shape[0]))[None, :]


def kernel(x, p, positions, norm_mix, w_in, b_gate, moba_q_norm, moba_k_norm, dil_q_norm, dil_k_norm,
           ssm_conv_w, ssm_conv_b, ssm_dt_bias, ssm_a_log, ssm_d, ssm_out_norm, w_br_moba, w_br_ssm,
           w_br_dil, w_out, norm_ffn, w_up, ffn_conv_w, ffn_conv_b, w_down, norm_ple, w_ple_gate, w_ple):
    bsz, seq, d = x.shape
    depth = w_in.shape[0]
    n = bsz * seq
    bd, ssd_consts = _constants()
    tables = rope_tables(positions)
    row = lambda v: v[None, :]
    two = lambda v: jnp.tile(v, 2)[None, :]

    x2 = x.reshape(n, d)
    for i in range(depth):
        w_main = jnp.concatenate([w_in[i][:, :COL_DT], w_in[i][:, COL_GATE:]], axis=1).astype(BF16)
        w_dt = jnp.pad(w_in[i][:, COL_DT:COL_GATE], ((0, 0), (0, LANES - SSM_HEADS))).astype(BF16)
        proj2, dt2 = in_projection(x2, row(norm_mix[i]), w_main, w_dt)
        proj3 = proj2.reshape(bsz, seq, PROJ_COLS)

        out_a = moba_attention(proj3, tables, two(moba_q_norm[i]), two(moba_k_norm[i]), bd)
        out_b = ssd_mixer(proj3, dt2.reshape(bsz, seq, LANES), ssm_conv_w[i], row(ssm_conv_b[i]),
                          _pad_lanes(ssm_dt_bias[i]), _pad_lanes(ssm_a_log[i]),
                          row(jnp.repeat(ssm_d[i], SSM_HEAD_DIM)), row(ssm_out_norm[i]), ssd_consts)
        dil = [dilated_group(proj3, tables, two(dil_q_norm[i]), two(dil_k_norm[i]), bd, g)
               for g in range(DIL_GROUPS)]
        dil2 = [(o.reshape(n, DIL_WIDTH), l.reshape(n, DIL_WIDTH)) for o, l in dil]

        x2 = merge_branches(x2, out_a.reshape(n, MOBA_WIDTH), out_b.reshape(n, SSM_INNER), dil2, proj2,
                            row(b_gate[i]), w_br_moba[i].astype(BF16), w_br_ssm[i].astype(BF16),
                            w_br_dil[i].astype(BF16), w_out[i].astype(BF16))
        x2 = ffn_ple(x2, p[i].reshape(n, PLE_DIM), row(norm_ffn[i]), w_up[i].astype(BF16), ffn_conv_w[i],
                     row(ffn_conv_b[i]), w_down[i].astype(BF16), row(norm_ple[i]),
                     w_ple_gate[i].astype(BF16), w_ple[i].astype(BF16), seq)
    return x2.reshape(bsz, seq, d)
```

```python
import functools
import math

import numpy as np
import jax
import jax.numpy as jnp
from jax import lax
from jax.experimental import pallas as pl
from jax.experimental.pallas import tpu as pltpu

F32 = jnp.float32
BF16 = jnp.bfloat16

D_MODEL = 1024
PLE_DIM = 256
HEAD_DIM = 64
ROPE_DIM = HEAD_DIM // 4
ROPE_THETA = 500000.0
NORM_EPS = 1e-6
NEG_INF = -1e30

MOBA_HEADS = 8
MOBA_BLOCK = 256
MOBA_TOPK = 3
MOBA_WIDTH = MOBA_HEADS * HEAD_DIM

DIL_RATES = (1, 4, 16)
DIL_GROUPS = 3
DIL_HEADS = 8
DIL_WINDOW = 128
DIL_WIDTH = DIL_HEADS * HEAD_DIM
DIL_QBLOCK = 128
DIL_UNROLL = 4

SSM_INNER = D_MODEL
SSM_HEAD_DIM = 64
SSM_HEADS = SSM_INNER // SSM_HEAD_DIM
SSM_GROUPS = 4
SSM_STATE = 128
SSM_CONV = 4
SSM_CHUNK = 128
SSM_XBC = SSM_INNER + 2 * SSM_GROUPS * SSM_STATE

FFN_DIM = 2816
FFN_CONV = 3
FFN_CHUNK = 256
N_BRANCH = 3

COL_MOBA = 0
COL_DIL = 3 * MOBA_WIDTH
COL_Z = COL_DIL + 3 * DIL_GROUPS * DIL_WIDTH
COL_XBC = COL_Z + SSM_INNER
COL_DT = COL_XBC + SSM_XBC
COL_GATE = COL_DT + SSM_HEADS
IN_COLS = COL_GATE + N_BRANCH * D_MODEL
IN_TILE = 1536
ATT_TILES = COL_Z // IN_TILE
MAIN_TILES = COL_DT // IN_TILE
GATE_TILES = N_BRANCH * D_MODEL // IN_TILE
MIX_Z = 0
MIX_XBC = SSM_INNER
MIX_GATE = SSM_INNER + SSM_XBC
MIX_COLS = MIX_GATE + N_BRANCH * D_MODEL

QSCALE = HEAD_DIM ** -0.5 * math.log2(math.e)

LANES = 128
HALO = 16
VMEM_LIMIT = 56 * 1024 * 1024


def _params(*sem):
    return pltpu.CompilerParams(dimension_semantics=sem, vmem_limit_bytes=VMEM_LIMIT)


def _sigmoid(x):
    return 1.0 / (1.0 + jnp.exp2(x * -math.log2(math.e)))


def _silu(x):
    return x * _sigmoid(x)


def _split3(a):
    a1 = a.astype(BF16)
    r1 = a - a1.astype(F32)
    a2 = r1.astype(BF16)
    a3 = (r1 - a2.astype(F32)).astype(BF16)
    return a1, a2, a3


def _dot_exact_rhs(a, b_exact, passes=3):
    out = None
    for piece in _split3(a)[:passes]:
        t = jnp.dot(piece, b_exact, preferred_element_type=F32)
        out = t if out is None else out + t
    return out


def _dot_exact_lhs(a_exact, b, passes=3):
    out = None
    for piece in _split3(b)[:passes]:
        t = jnp.dot(a_exact, piece, preferred_element_type=F32)
        out = t if out is None else out + t
    return out


def _head_norm_rope(x, gain, bd, cos, sin_up, sin_dn):
    ms = _dot_exact_rhs(x * x, bd, passes=2)
    y = x * lax.rsqrt(ms + NORM_EPS) * gain
    half = ROPE_DIM // 2
    return y * cos + pltpu.roll(y, half, 1) * sin_up + pltpu.roll(y, LANES - half, 1) * sin_dn


def _rope_kernel(pos_ref, inv_ref, cos_ref, up_ref, dn_ref):
    ang = pos_ref[0] * inv_ref[...]
    d = lax.broadcasted_iota(jnp.int32, ang.shape, 1) % HEAD_DIM
    half = ROPE_DIM // 2
    s = jnp.sin(ang)
    cos_ref[0] = jnp.cos(ang)
    up_ref[0] = jnp.where((d >= half) & (d < ROPE_DIM), s, 0.0)
    dn_ref[0] = jnp.where(d < half, -s, 0.0)


def rope_tables(positions):
    bsz, seq = positions.shape
    ts = min(seq, 1024)
    d = np.arange(LANES) % HEAD_DIM
    inv = ROPE_THETA ** (-jnp.arange(0, ROPE_DIM, 2, dtype=F32) / ROPE_DIM)
    inv_lane = jnp.where(d < ROPE_DIM, inv[d % (ROPE_DIM // 2)], 0.0).astype(F32)[None, :]
    pos = positions.astype(F32)[..., None]
    shp = jax.ShapeDtypeStruct((bsz, seq, LANES), F32)
    spec = pl.BlockSpec((1, ts, LANES), lambda b, t: (b, t, 0))
    return pl.pallas_call(
        _rope_kernel, out_shape=(shp, shp, shp), grid=(bsz, seq // ts),
        in_specs=[pl.BlockSpec((1, ts, 1), lambda b, t: (b, t, 0)),
                  pl.BlockSpec((1, LANES), lambda b, t: (0, 0))],
        out_specs=(spec, spec, spec), compiler_params=_params("parallel", "parallel"),
        name="rope_tables")(pos, inv_lane)


def _inproj_kernel(x_ref, g_ref, w_ref, wg_ref, wdt_ref, att_ref, mix_ref, dt_ref, u_ref):
    j = pl.program_id(1)

    @pl.when(j == 0)
    def _():
        u = _rms_bf16(x_ref[...], g_ref[...])
        u_ref[...] = u
        dt_ref[...] = jnp.dot(u, wdt_ref[...], preferred_element_type=F32)

    @pl.when(j < ATT_TILES)
    def _():
        att_ref[...] = jnp.dot(u_ref[...], w_ref[...], preferred_element_type=F32)

    @pl.when((j >= ATT_TILES) & (j < MAIN_TILES))
    def _():
        mix_ref[...] = jnp.dot(u_ref[...], w_ref[...], preferred_element_type=F32).astype(BF16)

    @pl.when(j >= MAIN_TILES)
    def _():
        mix_ref[...] = jnp.dot(u_ref[...], wg_ref[...], preferred_element_type=F32).astype(BF16)


def in_projection(x2, gain, w_all, w_gate, w_dt, tm=1024):
    n, d = x2.shape
    tm = min(tm, n)
    tn = IN_TILE
    return pl.pallas_call(
        _inproj_kernel,
        out_shape=(jax.ShapeDtypeStruct((n, COL_Z), F32), jax.ShapeDtypeStruct((n, MIX_COLS), BF16),
                   jax.ShapeDtypeStruct((n, LANES), F32)),
        grid=(n // tm, MAIN_TILES + GATE_TILES),
        in_specs=[pl.BlockSpec((tm, d), lambda i, j: (i, 0)),
                  pl.BlockSpec((1, d), lambda i, j: (0, 0)),
                  pl.BlockSpec((d, tn), lambda i, j: (0, jnp.minimum(j, MAIN_TILES - 1))),
                  pl.BlockSpec((d, tn), lambda i, j: (0, jnp.maximum(j - MAIN_TILES, 0))),
                  pl.BlockSpec((d, LANES), lambda i, j: (0, 0))],
        out_specs=(pl.BlockSpec((tm, tn), lambda i, j: (i, jnp.minimum(j, ATT_TILES - 1))),
                   pl.BlockSpec((tm, tn), lambda i, j: (i, jnp.maximum(j - ATT_TILES, 0))),
                   pl.BlockSpec((tm, LANES), lambda i, j: (i, 0))),
        scratch_shapes=[pltpu.VMEM((tm, d), BF16)],
        compiler_params=_params("parallel", "arbitrary"), name="in_projection",
    )(x2, gain, w_all, w_gate, w_dt)


def _head_lanes(shape):
    lane = lax.broadcasted_iota(jnp.int32, shape, len(shape) - 1)
    return lane, (lane < HEAD_DIM, lane >= HEAD_DIM), (HEAD_DIM, 0)


def _moba_kernel(q_ref, k_ref, v_ref, cos_ref, up_ref, dn_ref, gq_ref, gk_ref, bd_ref, o_ref,
                 qa_ref, ka_ref, va_ref, qf_ref, km_ref, *, nb):
    blk = MOBA_BLOCK
    bd = bd_ref[...]
    lane, head, aux = _head_lanes((blk, LANES))
    nbp = km_ref.shape[0]
    km_ref[...] = jnp.zeros(km_ref.shape, F32)

    def prep(j):
        rows = pl.ds(pl.multiple_of(j * blk, blk), blk)
        cos, up, dn = cos_ref[0, rows, :], up_ref[0, rows, :], dn_ref[0, rows, :]
        qf_ref[rows, :] = _head_norm_rope(q_ref[0, rows, :], gq_ref[...], bd, cos, up, dn)
        kn = _head_norm_rope(k_ref[0, rows, :], gk_ref[...], bd, cos, up, dn)
        v = v_ref[0, rows, :]
        km_ref[pl.ds(j, 1), :] = jnp.mean(kn, axis=0, keepdims=True)
        for h in range(2):
            ka_ref[h, rows, :] = jnp.where(head[h], kn, jnp.where(lane == aux[h] + j, 1.0, 0.0)).astype(BF16)
            va_ref[h, rows, :] = jnp.where(head[h], v, jnp.where(lane == aux[h], 1.0, 0.0)).astype(BF16)

    def prep2(t, carry):
        prep(2 * t)
        prep(2 * t + 1)
        return carry

    lax.fori_loop(0, nb // 2, prep2, 0)

    bidx = lax.broadcasted_iota(jnp.int32, (nbp, blk), 0).astype(F32)
    isblk = bidx < float(nb)
    _, head_k, _ = _head_lanes((nbp, LANES))
    nt = (((1,), (1,)), ((), ()))

    def select(i):
        rows = pl.ds(pl.multiple_of(i * blk, blk), blk)
        qf = qf_ref[rows, :]
        q_hi = qf.astype(BF16)
        q_lo = (qf - q_hi.astype(F32)).astype(BF16)
        i_f = lax.convert_element_type(i, F32)
        for h in range(2):
            km = jnp.where(head_k[h], km_ref[...], 0.0)
            k_hi = km.astype(BF16)
            k_lo = (km - k_hi.astype(F32)).astype(BF16)
            dot = lambda a, b: lax.dot_general(a, b, nt, preferred_element_type=F32)
            sc = dot(k_hi, q_hi) + (dot(k_hi, q_lo) + dot(k_lo, q_hi))
            valid = isblk & (bidx < i_f)
            cur = jnp.where(valid, sc, -jnp.inf)
            sel = jnp.zeros((nbp, blk), F32)
            for _ in range(min(MOBA_TOPK, nb)):
                mx = jnp.max(cur, axis=0, keepdims=True)
                first = jnp.min(jnp.where((cur == mx) & isblk, bidx, float(nbp)), axis=0, keepdims=True)
                hit = bidx == first
                sel = jnp.where(hit, 1.0, sel)
                cur = jnp.where(hit, -jnp.inf, cur)
            keep = ((sel > 0.5) & valid) | (bidx == i_f)
            bias = jnp.where(isblk & jnp.logical_not(keep), NEG_INF, 0.0)
            pieces = [jnp.zeros((aux[h], blk), F32)] if aux[h] else []
            pieces += [bias, jnp.zeros((LANES - aux[h] - nbp, blk), F32)]
            bias_t = jnp.concatenate(pieces, axis=0).T
            qa_ref[h, rows, :] = jnp.where(head[h], qf * QSCALE, bias_t).astype(BF16)

    def select2(t, carry):
        select(2 * t)
        select(2 * t + 1)
        return carry

    lax.fori_loop(0, nb // 2, select2, 0)

    wide = 2 * blk
    _, head_w, _ = _head_lanes((wide, LANES))
    causal = (lax.broadcasted_iota(jnp.int32, (wide, wide), 1)
              <= lax.broadcasted_iota(jnp.int32, (wide, wide), 0))
    nt = (((1,), (1,)), ((), ()))

    def rows_of(t):
        return slice(t * wide, (t + 1) * wide)

    def logits(a, g):
        return [lax.dot_general(qa_ref[h, rows_of(a), :], ka_ref[h, rows_of(g), :], nt,
                                preferred_element_type=F32) for h in range(2)]

    tiles = [(a, g) for a in range(nb // 2) for g in [a] + list(range(a))]
    ss = logits(*tiles[0])
    state = None
    for t, (a, g) in enumerate(tiles):
        nxt = logits(*tiles[t + 1]) if t + 1 < len(tiles) else None
        new = []
        for h in range(2):
            s = ss[h]
            if g == a:
                s = jnp.where(causal, s, NEG_INF)
                m = jnp.max(s, axis=-1, keepdims=True)
                acc = jnp.dot(jnp.exp2(s - m).astype(BF16), va_ref[h, rows_of(g), :],
                              preferred_element_type=F32)
            else:
                m_old, acc_old = state[h]
                m = jnp.maximum(m_old, jnp.max(s, axis=-1, keepdims=True))
                acc = jnp.exp2(m_old - m) * acc_old + jnp.dot(
                    jnp.exp2(s - m).astype(BF16), va_ref[h, rows_of(g), :], preferred_element_type=F32)
            new.append((m, acc))
        state, ss = new, nxt
        if g == a - 1 or a == 0:
            den0 = state[0][1][:, aux[0]:aux[0] + 1]
            den1 = state[1][1][:, aux[1]:aux[1] + 1]
            o_ref[0, rows_of(a), :] = jnp.where(head_w[0], state[0][1] / den0, state[1][1] / den1)


def moba_attention(proj3, tables, gq, gk, bd):
    bsz, seq, _ = proj3.shape
    nb = seq // MOBA_BLOCK
    assert nb % 2 == 0 and nb <= HEAD_DIM, "key blocks are visited in pairs and indexed on 64 spare lanes"
    hp = MOBA_WIDTH // LANES
    cos, up, dn = tables
    qkv = lambda part: pl.BlockSpec((1, seq, LANES), lambda b, p, part=part: (b, 0, part * hp + p))
    tab = pl.BlockSpec((1, seq, LANES), lambda b, p: (b, 0, 0))
    vec = pl.BlockSpec((1, LANES), lambda b, p: (0, 0))
    return pl.pallas_call(
        functools.partial(_moba_kernel, nb=nb),
        out_shape=jax.ShapeDtypeStruct((bsz, seq, MOBA_WIDTH), F32), grid=(bsz, hp),
        in_specs=[qkv(0), qkv(1), qkv(2), tab, tab, tab, vec, vec,
                  pl.BlockSpec((LANES, LANES), lambda b, p: (0, 0))],
        out_specs=pl.BlockSpec((1, seq, LANES), lambda b, p: (b, 0, p)),
        scratch_shapes=[pltpu.VMEM((2, seq, LANES), BF16), pltpu.VMEM((2, seq, LANES), BF16),
                        pltpu.VMEM((2, seq, LANES), BF16), pltpu.VMEM((seq, LANES), F32),
                        pltpu.VMEM((-(-nb // 8) * 8, LANES), F32)],
        compiler_params=_params("parallel", "parallel"), name="moba_attention",
    )(proj3, proj3, proj3, cos, up, dn, gq, gk, bd)


def _dilated_kernel(q_ref, k_ref, v_ref, cos_ref, up_ref, dn_ref, gq_ref, gk_ref, bd_ref,
                    o_ref, lse_ref, qd_ref, kd_ref, va_ref, *, seq, rate):
    qb = DIL_QBLOCK
    cpb = seq // rate // qb
    shift = cpb.bit_length() - 1
    bd = bd_ref[...]
    lane, head, aux = _head_lanes((qb, LANES))

    def token_rows(n):
        if rate == 1:
            return pl.ds(pl.multiple_of(n * qb, qb), qb)
        c, ch = lax.shift_right_logical(n, shift), n & (cpb - 1)
        return pl.ds(c + ch * (qb * rate), qb, stride=rate)

    def prep(n):
        src = token_rows(n)
        dst = pl.ds(pl.multiple_of(n * qb, qb), qb)
        cos, up, dn = cos_ref[0, src, :], up_ref[0, src, :], dn_ref[0, src, :]
        qn = _head_norm_rope(q_ref[0, src, :], gq_ref[...], bd, cos, up, dn)
        kn = _head_norm_rope(k_ref[0, src, :], gk_ref[...], bd, cos, up, dn)
        v = v_ref[0, src, :]
        qd_ref[dst, :] = (qn * QSCALE).astype(BF16)
        for h in range(2):
            kd_ref[h, dst, :] = jnp.where(head[h], kn, 0.0).astype(BF16)
            va_ref[h, dst, :] = jnp.concatenate(
                [jnp.where(head[h], v, 0.0), jnp.where(head[h], 1.0, 0.0)], axis=1).astype(BF16)

    def unrolled(fn):
        def step(it, carry):
            for u in range(DIL_UNROLL):
                fn(it * DIL_UNROLL + u)
            return carry
        lax.fori_loop(0, seq // qb // DIL_UNROLL, step, 0)

    unrolled(prep)

    rel = (lax.broadcasted_iota(jnp.int32, (qb, 2 * qb), 0)
           - lax.broadcasted_iota(jnp.int32, (qb, 2 * qb), 1))
    nt = (((1,), (1,)), ((), ()))

    def qblock(n):
        first = (n & (cpb - 1)) == 0
        k0 = jnp.where(first, n, n - 1) * qb
        krows = pl.ds(pl.multiple_of(k0, qb), 2 * qb)
        qrows = pl.ds(pl.multiple_of(n * qb, qb), qb)
        dist = rel + (n * qb - k0)
        ok = (dist >= 0) & (dist <= DIL_WINDOW)
        keys = jnp.concatenate([kd_ref[0, krows, :], kd_ref[1, krows, :]], axis=0)
        s = lax.dot_general(qd_ref[qrows, :], keys, nt, preferred_element_type=F32)
        ms, ps = [], []
        for h in range(2):
            sh = jnp.where(ok, s[:, h * 2 * qb:(h + 1) * 2 * qb], NEG_INF)
            ms.append(jnp.max(sh, axis=-1, keepdims=True))
            ps.append(jnp.exp2(sh - ms[h]).astype(BF16))
        vals = jnp.concatenate([va_ref[0, krows, :], va_ref[1, krows, :]], axis=0)
        acc = jnp.dot(jnp.concatenate(ps, axis=1), vals, preferred_element_type=F32)
        den = acc[:, LANES:]
        dst = token_rows(n)
        o_ref[0, dst, :] = acc[:, :LANES] / den
        lse_ref[0, dst, :] = jnp.where(head[0], ms[0], ms[1]) + jnp.log2(den)

    unrolled(qblock)


def dilated_group(proj3, tables, gq, gk, bd, group):
    bsz, seq, cols = proj3.shape
    rate = DIL_RATES[group]
    cpb = seq // rate // DIL_QBLOCK
    assert cpb >= 2 and cpb & (cpb - 1) == 0 and (seq // DIL_QBLOCK) % DIL_UNROLL == 0
    hp = DIL_WIDTH // LANES
    base = COL_DIL // LANES
    qkv = lambda part: pl.BlockSpec(
        (1, seq, LANES), lambda b, p, part=part: (b, 0, base + (part * DIL_GROUPS + group) * hp + p))
    tab = pl.BlockSpec((1, seq, LANES), lambda b, p: (b, 0, 0))
    vec = pl.BlockSpec((1, LANES), lambda b, p: (0, 0))
    out = pl.BlockSpec((1, seq, LANES), lambda b, p: (b, 0, p))
    shp = jax.ShapeDtypeStruct((bsz, seq, DIL_WIDTH), F32)
    return pl.pallas_call(
        functools.partial(_dilated_kernel, seq=seq, rate=rate), out_shape=(shp, shp), grid=(bsz, hp),
        in_specs=[qkv(0), qkv(1), qkv(2), tab, tab, tab, vec, vec,
                  pl.BlockSpec((LANES, LANES), lambda b, p: (0, 0))],
        out_specs=(out, out),
        scratch_shapes=[pltpu.VMEM((seq, LANES), BF16), pltpu.VMEM((2, seq, LANES), BF16),
                        pltpu.VMEM((2, seq, 2 * LANES), BF16)],
        compiler_params=_params("parallel", "parallel"), name=f"dilated_rate{rate}",
    )(proj3, proj3, proj3, *tables, gq, gk, bd)


def _ssd_kernel(z_ref, xs_ref, bc_ref, dt_ref, cwx_ref, cwb_ref, cbx_ref, cbb_ref, dtb_ref, alog_ref,
                dexp_ref, onorm_ref, tri_ref, triu_ref, exp_ref, o_ref, xpx_ref, xpb_ref, st_ref):
    L = SSM_CHUNK
    pad = 8

    @pl.when(pl.program_id(1) == 0)
    def _():
        xpx_ref[0:pad, :] = jnp.zeros((pad, SSM_INNER), F32)
        xpb_ref[0:pad, :] = jnp.zeros((pad, SSM_INNER), F32)
        st_ref[...] = jnp.zeros(st_ref.shape, F32)

    def conv_silu(src_ref, pad_ref, w_ref, b_ref):
        pad_ref[pad:, :] = src_ref[0].astype(F32)
        acc = b_ref[...] + w_ref[0:1, :] * pad_ref[pl.ds(pad - SSM_CONV + 1, L), :]
        for k in range(1, SSM_CONV):
            acc = acc + w_ref[k:k + 1, :] * pad_ref[pl.ds(pad - SSM_CONV + 1 + k, L), :]
        pad_ref[0:pad, :] = pad_ref[L:L + pad, :]
        return _silu(acc)

    xs = conv_silu(xs_ref, xpx_ref, cwx_ref, cbx_ref)
    bc = conv_silu(bc_ref, xpb_ref, cwb_ref, cbb_ref)
    gn = SSM_GROUPS * SSM_STATE
    bm, cm = bc[:, :gn], bc[:, gn:]

    xr = dt_ref[0] + dtb_ref[...]
    dt = jnp.maximum(xr, 0.0) + jnp.log(1.0 + jnp.exp(-jnp.abs(xr)))
    adt = dt * (-jnp.exp(alog_ref[...]))
    acs = _dot_exact_lhs(tri_ref[...], adt)
    acs_t = _dot_exact_rhs(adt.T, triu_ref[...])
    expand = exp_ref[...]
    dt_e = _dot_exact_rhs(dt, expand)
    acs_e = _dot_exact_rhs(acs, expand)
    xdt = xs * dt_e
    last = acs_e[L - 1:L, :]
    grow = jnp.exp(acs_e)
    to_end = jnp.exp(last - acs_e)
    chunk_decay = jnp.exp(last)

    ll = lax.broadcasted_iota(jnp.int32, (L, L), 0)
    ss = lax.broadcasted_iota(jnp.int32, (L, L), 1)
    causal = ll >= ss
    gw = SSM_INNER // SSM_GROUPS
    hpg = SSM_HEADS // SSM_GROUPS
    lane = lax.broadcasted_iota(jnp.int32, (L, gw), 1)
    nt = (((1,), (1,)), ((), ()))
    tn = (((0,), (0,)), ((), ()))
    ys = []
    for g in range(SSM_GROUPS):
        cols = slice(g * gw, (g + 1) * gw)
        bg = bm[:, g * SSM_STATE:(g + 1) * SSM_STATE].astype(BF16)
        cg = cm[:, g * SSM_STATE:(g + 1) * SSM_STATE].astype(BF16)
        xg = xdt[:, cols]
        cb = lax.dot_general(cg, bg, nt, preferred_element_type=F32)
        st = st_ref[g]
        y = jnp.dot(cg, st.astype(BF16), preferred_element_type=F32) * grow[:, cols]
        new = lax.dot_general(bg, (xg * to_end[:, cols]).astype(BF16), tn, preferred_element_type=F32)
        st_ref[g] = chunk_decay[:, cols] * st + new
        for hh in range(hpg):
            h = g * hpg + hh
            diff = acs[:, h:h + 1] - acs_t[h:h + 1, :]
            mat = (cb * jnp.exp(jnp.where(causal, diff, -jnp.inf))).astype(BF16)
            xh = jnp.where((lane >= hh * SSM_HEAD_DIM) & (lane < (hh + 1) * SSM_HEAD_DIM), xg, 0.0)
            y = y + jnp.dot(mat, xh.astype(BF16), preferred_element_type=F32)
        ys.append(y)
    y = jnp.concatenate(ys, axis=1) + xs * dexp_ref[...]
    yg = y * _silu(z_ref[0].astype(F32))
    ms = jnp.mean(yg * yg, axis=-1, keepdims=True)
    o_ref[0] = yg * lax.rsqrt(ms + NORM_EPS) * onorm_ref[...]


def ssd_mixer(mix3, dt3, conv_w, conv_b, dt_bias, a_log, d_exp, out_norm, consts):
    bsz, seq, _ = mix3.shape
    L = SSM_CHUNK
    w = SSM_INNER
    tri, triu, expand = consts
    col = lambda idx: pl.BlockSpec((1, L, w), lambda b, c, idx=idx: (b, c, idx))
    vecw = lambda rows, idx: pl.BlockSpec((rows, w), lambda b, c, idx=idx: (0, idx))
    vec = pl.BlockSpec((1, LANES), lambda b, c: (0, 0))
    sq = pl.BlockSpec((L, L), lambda b, c: (0, 0))
    return pl.pallas_call(
        _ssd_kernel, out_shape=jax.ShapeDtypeStruct((bsz, seq, w), F32), grid=(bsz, seq // L),
        in_specs=[col(MIX_Z // w), col(MIX_XBC // w), col(MIX_XBC // w + 1),
                  pl.BlockSpec((1, L, LANES), lambda b, c: (b, c, 0)),
                  vecw(SSM_CONV, 0), vecw(SSM_CONV, 1), vecw(1, 0), vecw(1, 1), vec, vec,
                  vecw(1, 0), vecw(1, 0), sq, sq, pl.BlockSpec((LANES, w), lambda b, c: (0, 0))],
        out_specs=pl.BlockSpec((1, L, w), lambda b, c: (b, c, 0)),
        scratch_shapes=[pltpu.VMEM((L + 8, w), F32), pltpu.VMEM((L + 8, w), F32),
                        pltpu.VMEM((SSM_GROUPS, SSM_STATE, w // SSM_GROUPS), F32)],
        compiler_params=_params("parallel", "arbitrary"), name="ssd_mixer",
    )(mix3, mix3, mix3, dt3, conv_w, conv_w, conv_b, conv_b, dt_bias, a_log, d_exp, out_norm,
      tri, triu, expand)


def _merge_kernel(x_ref, a_ref, m_ref, o0_ref, o1_ref, o2_ref, l0_ref, l1_ref, l2_ref, gl_ref, bg_ref,
                  wa_ref, wm_ref, wc_ref, wo_ref, out_ref):
    l0, l1, l2 = l0_ref[...], l1_ref[...], l2_ref[...]
    lmax = jnp.maximum(jnp.maximum(l0, l1), l2)
    e0, e1, e2 = jnp.exp2(l0 - lmax), jnp.exp2(l1 - lmax), jnp.exp2(l2 - lmax)
    cmix = (e0 * o0_ref[...] + e1 * o1_ref[...] + e2 * o2_ref[...]) / (e0 + e1 + e2)
    gates = _sigmoid(gl_ref[...] + bg_ref[...])
    d = D_MODEL
    mm = lambda v, w_ref: jnp.dot(v.astype(BF16), w_ref[...], preferred_element_type=F32)
    merged = (gates[:, :d] * mm(a_ref[...], wa_ref) + gates[:, d:2 * d] * mm(m_ref[...], wm_ref)
              + gates[:, 2 * d:] * mm(cmix, wc_ref))
    out_ref[...] = x_ref[...] + mm(merged, wo_ref)


def merge_branches(x2, a2, m2, dil, mix2, b_gate, wa, wm, wc, wo, tm=512):
    n, d = x2.shape
    tm = min(tm, n)
    row = lambda width, idx=0: pl.BlockSpec((tm, width), lambda i, idx=idx: (i, idx))
    full = lambda arr: pl.BlockSpec(arr.shape, lambda i: (0, 0), pipeline_mode=pl.Buffered(1))
    (o0, l0), (o1, l1), (o2, l2) = dil
    gw = N_BRANCH * d
    return pl.pallas_call(
        _merge_kernel, out_shape=jax.ShapeDtypeStruct((n, d), F32), grid=(n // tm,),
        in_specs=[row(d), row(MOBA_WIDTH), row(SSM_INNER)] + [row(DIL_WIDTH)] * 6
                 + [row(gw, MIX_GATE // gw), full(b_gate), full(wa), full(wm), full(wc), full(wo)],
        out_specs=row(d), compiler_params=_params("parallel"), name="merge_branches",
    )(x2, a2, m2, o0, o1, o2, l0, l1, l2, mix2, b_gate, wa, wm, wc, wo)


def _rms_bf16(x, gain):
    ms = jnp.mean(x * x, axis=-1, keepdims=True)
    return (x * lax.rsqrt(ms + NORM_EPS) * gain).astype(BF16)


def _ffn_ple_kernel(x_ref, xh_ref, p_ref, gf_ref, wup_ref, cw_ref, cb_ref, wd_ref, gp_ref, wg_ref, wp_ref,
                    o_ref, u_ref, buf_ref, h_ref, *, tm, tiles_per_seq):
    ck = FFN_CHUNK
    nck = FFN_DIM // ck
    u_ref[0:HALO, :] = _rms_bf16(xh_ref[...], gf_ref[...])
    u_ref[HALO:, :] = _rms_bf16(x_ref[...], gf_ref[...])
    keep = jnp.where(pl.program_id(0) % tiles_per_seq == 0, 0.0, 1.0)

    def up(c):
        u = u_ref[...]
        for half in range(2):
            buf = buf_ref.at[2 * (c % 2) + half]
            cols = slice(half * FFN_DIM + c * ck, half * FFN_DIM + (c + 1) * ck)
            val = jnp.dot(u, wup_ref[:, cols], preferred_element_type=F32)
            buf[0:HALO, :] = val[0:HALO, :] * keep
            buf[HALO:, :] = val[HALO:, :]

    def conv(c, half):
        buf = buf_ref.at[2 * (c % 2) + half]
        cols = slice(half * FFN_DIM + c * ck, half * FFN_DIM + (c + 1) * ck)
        acc = cb_ref[:, cols] + cw_ref[0:1, cols] * buf[pl.ds(HALO - FFN_CONV + 1, tm), :]
        for k in range(1, FFN_CONV):
            acc = acc + cw_ref[k:k + 1, cols] * buf[pl.ds(HALO - FFN_CONV + 1 + k, tm), :]
        return acc

    up(0)
    for c in range(nck):
        if c + 1 < nck:
            up(c + 1)
        h_ref[:, c * ck:(c + 1) * ck] = (_silu(conv(c, 0)) * conv(c, 1)).astype(BF16)
    acc = x_ref[...] + jnp.dot(h_ref[...], wd_ref[...], preferred_element_type=F32)
    pg = _sigmoid(jnp.dot(_rms_bf16(acc, gp_ref[...]), wg_ref[...], preferred_element_type=F32))
    o_ref[...] = acc + jnp.dot(p_ref[...].astype(BF16), wp_ref[...], preferred_element_type=F32) * pg


def ffn_ple(x2, p2, gain_ffn, w_up, conv_w, conv_b, w_down, gain_ple, w_gate, w_ple, seq, tm=512):
    n, d = x2.shape
    tm = min(tm, seq)
    hb = tm // HALO
    row = lambda width: pl.BlockSpec((tm, width), lambda i: (i, 0))
    resident = lambda arr: pl.BlockSpec(arr.shape, lambda i: (0, 0), pipeline_mode=pl.Buffered(1))
    return pl.pallas_call(
        functools.partial(_ffn_ple_kernel, tm=tm, tiles_per_seq=seq // tm),
        out_shape=jax.ShapeDtypeStruct((n, d), F32), grid=(n // tm,),
        in_specs=[row(d), pl.BlockSpec((HALO, d), lambda i: (jnp.maximum(i * hb - 1, 0), 0)), row(PLE_DIM),
                  resident(gain_ffn), resident(w_up), resident(conv_w), resident(conv_b), resident(w_down),
                  resident(gain_ple), resident(w_gate), resident(w_ple)],
        out_specs=row(d),
        scratch_shapes=[pltpu.VMEM((tm + HALO, d), BF16), pltpu.VMEM((4, tm + HALO, FFN_CHUNK), F32),
                        pltpu.VMEM((tm, FFN_DIM), BF16)],
        compiler_params=_params("parallel"), name="ffn_ple",
    )(x2, x2, p2, gain_ffn, w_up, conv_w, conv_b, w_down, gain_ple, w_gate, w_ple)


def _constants():
    lane = np.arange(LANES)
    bd = (lane[:, None] // HEAD_DIM == lane[None, :] // HEAD_DIM).astype(np.float32) / HEAD_DIM
    r = np.arange(SSM_CHUNK)
    tri = (r[None, :] <= r[:, None]).astype(np.float32)
    expand = (lane[:, None] == (np.arange(SSM_INNER)[None, :] // SSM_HEAD_DIM)).astype(np.float32)
    as_bf16 = lambda a: jnp.asarray(a, dtype=BF16)
    return as_bf16(bd), (as_bf16(tri), as_bf16(tri.T), as_bf16(expand))


def _pad_lanes(v):
    return jnp.pad(v, (0, LANES - v.shape[0]))[None, :]


def kernel(x, p, positions, norm_mix, w_in, b_gate, moba_q_norm, moba_k_norm, dil_q_norm, dil_k_norm,
           ssm_conv_w, ssm_conv_b, ssm_dt_bias, ssm_a_log, ssm_d, ssm_out_norm, w_br_moba, w_br_ssm,
           w_br_dil, w_out, norm_ffn, w_up, ffn_conv_w, ffn_conv_b, w_down, norm_ple, w_ple_gate, w_ple):
    bsz, seq, d = x.shape
    depth = w_in.shape[0]
    n = bsz * seq
    bd, ssd_consts = _constants()
    tables = rope_tables(positions)
    row = lambda v: v[None, :]
    two = lambda v: jnp.tile(v, 2)[None, :]

    x2 = x.reshape(n, d)
    for i in range(depth):
        w_all = w_in[i].astype(BF16)
        w_dt = jnp.pad(w_all[:, COL_DT:COL_GATE], ((0, 0), (0, LANES - SSM_HEADS)))
        att2, mix2, dt2 = in_projection(x2, row(norm_mix[i]), w_all, w_all[:, COL_GATE:], w_dt)
        att3 = att2.reshape(bsz, seq, COL_Z)

        out_a = moba_attention(att3, tables, two(moba_q_norm[i]), two(moba_k_norm[i]), bd)
        out_b = ssd_mixer(mix2.reshape(bsz, seq, MIX_COLS), dt2.reshape(bsz, seq, LANES), ssm_conv_w[i],
                          row(ssm_conv_b[i]), _pad_lanes(ssm_dt_bias[i]), _pad_lanes(ssm_a_log[i]),
                          row(jnp.repeat(ssm_d[i], SSM_HEAD_DIM)), row(ssm_out_norm[i]), ssd_consts)
        dil = [dilated_group(att3, tables, two(dil_q_norm[i]), two(dil_k_norm[i]), bd, g)
               for g in range(DIL_GROUPS)]
        dil2 = [(o.reshape(n, DIL_WIDTH), l.reshape(n, DIL_WIDTH)) for o, l in dil]

        x2 = merge_branches(x2, out_a.reshape(n, MOBA_WIDTH), out_b.reshape(n, SSM_INNER), dil2, mix2,
                            row(b_gate[i]), w_br_moba[i].astype(BF16), w_br_ssm[i].astype(BF16),
                            w_br_dil[i].astype(BF16), w_out[i].astype(BF16))
        x2 = ffn_ple(x2, p[i].reshape(n, PLE_DIM), row(norm_ffn[i]), w_up[i].astype(BF16), ffn_conv_w[i],
                     row(ffn_conv_b[i]), w_down[i].astype(BF16), row(norm_ple[i]),
                     w_ple_gate[i].astype(BF16), w_ple[i].astype(BF16), seq)
    return x2.reshape(bsz, seq, d)
```

```python
import functools
import math

import numpy as np
import jax
import jax.numpy as jnp
from jax import lax
from jax.experimental import pallas as pl
from jax.experimental.pallas import tpu as pltpu

F32 = jnp.float32
BF16 = jnp.bfloat16

D_MODEL = 1024
PLE_DIM = 256
HEAD_DIM = 64
ROPE_DIM = HEAD_DIM // 4
ROPE_THETA = 500000.0
NORM_EPS = 1e-6
NEG_INF = -1e30

MOBA_HEADS = 8
MOBA_BLOCK = 256
MOBA_TOPK = 3
MOBA_WIDTH = MOBA_HEADS * HEAD_DIM

DIL_RATES = (1, 4, 16)
DIL_GROUPS = 3
DIL_HEADS = 8
DIL_WINDOW = 128
DIL_WIDTH = DIL_HEADS * HEAD_DIM
DIL_QBLOCK = 128
DIL_UNROLL = 4

SSM_INNER = D_MODEL
SSM_HEAD_DIM = 64
SSM_HEADS = SSM_INNER // SSM_HEAD_DIM
SSM_GROUPS = 4
SSM_STATE = 128
SSM_CONV = 4
SSM_CHUNK = 128
SSM_XBC = SSM_INNER + 2 * SSM_GROUPS * SSM_STATE

FFN_DIM = 2816
FFN_CONV = 3
FFN_CHUNK = 256
N_BRANCH = 3

COL_MOBA = 0
COL_DIL = 3 * MOBA_WIDTH
COL_Z = COL_DIL + 3 * DIL_GROUPS * DIL_WIDTH
COL_XBC = COL_Z + SSM_INNER
COL_DT = COL_XBC + SSM_XBC
COL_GATE = COL_DT + SSM_HEADS
IN_COLS = COL_GATE + N_BRANCH * D_MODEL
IN_TILE = 1536
QK_ROWS = 256
ATT_TILES = COL_Z // IN_TILE
MAIN_TILES = COL_DT // IN_TILE
GATE_TILES = N_BRANCH * D_MODEL // IN_TILE
MIX_Z = 0
MIX_XBC = SSM_INNER
MIX_GATE = SSM_INNER + SSM_XBC
MIX_COLS = MIX_GATE + N_BRANCH * D_MODEL

QSCALE = HEAD_DIM ** -0.5 * math.log2(math.e)

LANES = 128
HALO = 16
VMEM_LIMIT = 56 * 1024 * 1024


def _params(*sem):
    return pltpu.CompilerParams(dimension_semantics=sem, vmem_limit_bytes=VMEM_LIMIT)


def _sigmoid(x):
    return 1.0 / (1.0 + jnp.exp2(x * -math.log2(math.e)))


def _silu(x):
    return x * _sigmoid(x)


def _split3(a):
    a1 = a.astype(BF16)
    r1 = a - a1.astype(F32)
    a2 = r1.astype(BF16)
    a3 = (r1 - a2.astype(F32)).astype(BF16)
    return a1, a2, a3


def _dot_exact_rhs(a, b_exact, passes=3):
    out = None
    for piece in _split3(a)[:passes]:
        t = jnp.dot(piece, b_exact, preferred_element_type=F32)
        out = t if out is None else out + t
    return out


def _dot_exact_lhs(a_exact, b, passes=3):
    out = None
    for piece in _split3(b)[:passes]:
        t = jnp.dot(a_exact, piece, preferred_element_type=F32)
        out = t if out is None else out + t
    return out


def _head_norm_rope(x, gain, bd, cos, sin_up, sin_dn):
    ms = _dot_exact_rhs(x * x, bd, passes=2)
    y = x * lax.rsqrt(ms + NORM_EPS) * gain
    half = ROPE_DIM // 2
    return y * cos + pltpu.roll(y, half, 1) * sin_up + pltpu.roll(y, LANES - half, 1) * sin_dn


def _rope_kernel(pos_ref, inv_ref, cos_ref, up_ref, dn_ref):
    ang = pos_ref[0] * inv_ref[...]
    d = lax.broadcasted_iota(jnp.int32, ang.shape, 1) % HEAD_DIM
    half = ROPE_DIM // 2
    s = jnp.sin(ang)
    cos_ref[0] = jnp.cos(ang)
    up_ref[0] = jnp.where((d >= half) & (d < ROPE_DIM), s, 0.0)
    dn_ref[0] = jnp.where(d < half, -s, 0.0)


def rope_tables(positions):
    bsz, seq = positions.shape
    ts = min(seq, 1024)
    d = np.arange(LANES) % HEAD_DIM
    inv = ROPE_THETA ** (-jnp.arange(0, ROPE_DIM, 2, dtype=F32) / ROPE_DIM)
    inv_lane = jnp.where(d < ROPE_DIM, inv[d % (ROPE_DIM // 2)], 0.0).astype(F32)[None, :]
    pos = positions.astype(F32)[..., None]
    shp = jax.ShapeDtypeStruct((bsz, seq, LANES), F32)
    spec = pl.BlockSpec((1, ts, LANES), lambda b, t: (b, t, 0))
    return pl.pallas_call(
        _rope_kernel, out_shape=(shp, shp, shp), grid=(bsz, seq // ts),
        in_specs=[pl.BlockSpec((1, ts, 1), lambda b, t: (b, t, 0)),
                  pl.BlockSpec((1, LANES), lambda b, t: (0, 0))],
        out_specs=(spec, spec, spec), compiler_params=_params("parallel", "parallel"),
        name="rope_tables")(pos, inv_lane)


def _inproj_kernel(x_ref, g_ref, w_ref, wg_ref, wdt_ref, hg_ref, bd_ref, cos_ref, up_ref, dn_ref,
                   att_ref, mix_ref, dt_ref, u_ref, *, tm):
    j = pl.program_id(1)

    @pl.when(j == 0)
    def _():
        u = _rms_bf16(x_ref[...], g_ref[...])
        u_ref[...] = u
        dt_ref[...] = jnp.dot(u, wdt_ref[...], preferred_element_type=F32)

    def attention_tile(normed_groups):
        att_ref[...] = jnp.dot(u_ref[...], w_ref[...], preferred_element_type=F32)
        bd = bd_ref[...]
        for r0 in range(0, tm, QK_ROWS):
            rows = slice(r0, r0 + QK_ROWS)
            cos, up, dn = cos_ref[rows, :], up_ref[rows, :], dn_ref[rows, :]
            for g in range(normed_groups):
                lanes = slice(g * LANES, (g + 1) * LANES)
                att_ref[rows, lanes] = _head_norm_rope(att_ref[rows, lanes], hg_ref[:, lanes], bd, cos, up, dn)

    @pl.when(j == 0)
    def _():
        attention_tile(2 * MOBA_WIDTH // LANES)

    @pl.when((j == 1) | (j == 2))
    def _():
        attention_tile(IN_TILE // LANES)

    @pl.when(j == 3)
    def _():
        attention_tile(0)

    @pl.when((j >= ATT_TILES) & (j < MAIN_TILES))
    def _():
        mix_ref[...] = jnp.dot(u_ref[...], w_ref[...], preferred_element_type=F32).astype(BF16)

    @pl.when(j >= MAIN_TILES)
    def _():
        mix_ref[...] = jnp.dot(u_ref[...], wg_ref[...], preferred_element_type=F32).astype(BF16)


def in_projection(x2, gain, w_all, w_gate, w_dt, head_gain, bd, tables, tm=1024):
    n, d = x2.shape
    tm = min(tm, n)
    tn = IN_TILE
    assert 3 * MOBA_WIDTH == tn and DIL_GROUPS * DIL_WIDTH == tn and tm % QK_ROWS == 0
    tab = pl.BlockSpec((tm, LANES), lambda i, j: (i, 0))
    return pl.pallas_call(
        functools.partial(_inproj_kernel, tm=tm),
        out_shape=(jax.ShapeDtypeStruct((n, COL_Z), F32), jax.ShapeDtypeStruct((n, MIX_COLS), BF16),
                   jax.ShapeDtypeStruct((n, LANES), F32)),
        grid=(n // tm, MAIN_TILES + GATE_TILES),
        in_specs=[pl.BlockSpec((tm, d), lambda i, j: (i, 0)),
                  pl.BlockSpec((1, d), lambda i, j: (0, 0)),
                  pl.BlockSpec((d, tn), lambda i, j: (0, jnp.minimum(j, MAIN_TILES - 1))),
                  pl.BlockSpec((d, tn), lambda i, j: (0, jnp.maximum(j - MAIN_TILES, 0))),
                  pl.BlockSpec((d, LANES), lambda i, j: (0, 0)),
                  pl.BlockSpec((1, tn), lambda i, j: (0, jnp.minimum(j, ATT_TILES - 1))),
                  pl.BlockSpec((LANES, LANES), lambda i, j: (0, 0)), tab, tab, tab],
        out_specs=(pl.BlockSpec((tm, tn), lambda i, j: (i, jnp.minimum(j, ATT_TILES - 1))),
                   pl.BlockSpec((tm, tn), lambda i, j: (i, jnp.maximum(j - ATT_TILES, 0))),
                   pl.BlockSpec((tm, LANES), lambda i, j: (i, 0))),
        scratch_shapes=[pltpu.VMEM((tm, d), BF16)],
        compiler_params=_params("parallel", "arbitrary"), name="in_projection",
    )(x2, gain, w_all, w_gate, w_dt, head_gain, bd, *tables)


def _head_lanes(shape):
    lane = lax.broadcasted_iota(jnp.int32, shape, len(shape) - 1)
    return lane, (lane < HEAD_DIM, lane >= HEAD_DIM), (HEAD_DIM, 0)


def _moba_kernel(q_ref, k_ref, v_ref, o_ref, qa_ref, ka_ref, va_ref, km_ref, *, nb):
    blk = MOBA_BLOCK
    lane, head, aux = _head_lanes((blk, LANES))
    nbp = km_ref.shape[0]
    km_ref[...] = jnp.zeros(km_ref.shape, F32)

    def prep(j):
        rows = pl.ds(pl.multiple_of(j * blk, blk), blk)
        kn = k_ref[0, rows, :]
        v = v_ref[0, rows, :]
        km_ref[pl.ds(j, 1), :] = jnp.mean(kn, axis=0, keepdims=True)
        for h in range(2):
            ka_ref[h, rows, :] = jnp.where(head[h], kn, jnp.where(lane == aux[h] + j, 1.0, 0.0)).astype(BF16)
            va_ref[h, rows, :] = jnp.where(head[h], v, jnp.where(lane == aux[h], 1.0, 0.0)).astype(BF16)

    def prep2(t, carry):
        prep(2 * t)
        prep(2 * t + 1)
        return carry

    lax.fori_loop(0, nb // 2, prep2, 0)

    bidx = lax.broadcasted_iota(jnp.int32, (nbp, blk), 0).astype(F32)
    isblk = bidx < float(nb)
    _, head_k, _ = _head_lanes((nbp, LANES))
    nt = (((1,), (1,)), ((), ()))

    def select(i):
        rows = pl.ds(pl.multiple_of(i * blk, blk), blk)
        qf = q_ref[0, rows, :]
        q_hi = qf.astype(BF16)
        q_lo = (qf - q_hi.astype(F32)).astype(BF16)
        i_f = lax.convert_element_type(i, F32)
        for h in range(2):
            km = jnp.where(head_k[h], km_ref[...], 0.0)
            k_hi = km.astype(BF16)
            k_lo = (km - k_hi.astype(F32)).astype(BF16)
            dot = lambda a, b: lax.dot_general(a, b, nt, preferred_element_type=F32)
            sc = dot(k_hi, q_hi) + (dot(k_hi, q_lo) + dot(k_lo, q_hi))
            valid = isblk & (bidx < i_f)
            cur = jnp.where(valid, sc, -jnp.inf)
            sel = jnp.zeros((nbp, blk), F32)
            for _ in range(min(MOBA_TOPK, nb)):
                mx = jnp.max(cur, axis=0, keepdims=True)
                first = jnp.min(jnp.where((cur == mx) & isblk, bidx, float(nbp)), axis=0, keepdims=True)
                hit = bidx == first
                sel = jnp.where(hit, 1.0, sel)
                cur = jnp.where(hit, -jnp.inf, cur)
            keep = ((sel > 0.5) & valid) | (bidx == i_f)
            bias = jnp.where(isblk & jnp.logical_not(keep), NEG_INF, 0.0)
            pieces = [jnp.zeros((aux[h], blk), F32)] if aux[h] else []
            pieces += [bias, jnp.zeros((LANES - aux[h] - nbp, blk), F32)]
            bias_t = jnp.concatenate(pieces, axis=0).T
            qa_ref[h, rows, :] = jnp.where(head[h], qf * QSCALE, bias_t).astype(BF16)

    def select2(t, carry):
        select(2 * t)
        select(2 * t + 1)
        return carry

    lax.fori_loop(0, nb // 2, select2, 0)

    wide = 2 * blk
    _, head_w, _ = _head_lanes((wide, LANES))
    causal = (lax.broadcasted_iota(jnp.int32, (wide, wide), 1)
              <= lax.broadcasted_iota(jnp.int32, (wide, wide), 0))
    nt = (((1,), (1,)), ((), ()))

    def rows_of(t):
        return slice(t * wide, (t + 1) * wide)

    def logits(a, g):
        return [lax.dot_general(qa_ref[h, rows_of(a), :], ka_ref[h, rows_of(g), :], nt,
                                preferred_element_type=F32) for h in range(2)]

    tiles = [(a, g) for a in range(nb // 2) for g in [a] + list(range(a))]
    ss = logits(*tiles[0])
    state = None
    for t, (a, g) in enumerate(tiles):
        nxt = logits(*tiles[t + 1]) if t + 1 < len(tiles) else None
        new = []
        for h in range(2):
            s = ss[h]
            if g == a:
                s = jnp.where(causal, s, NEG_INF)
                m = jnp.max(s, axis=-1, keepdims=True)
                acc = jnp.dot(jnp.exp2(s - m).astype(BF16), va_ref[h, rows_of(g), :],
                              preferred_element_type=F32)
            else:
                m_old, acc_old = state[h]
                m = jnp.maximum(m_old, jnp.max(s, axis=-1, keepdims=True))
                acc = jnp.exp2(m_old - m) * acc_old + jnp.dot(
                    jnp.exp2(s - m).astype(BF16), va_ref[h, rows_of(g), :], preferred_element_type=F32)
            new.append((m, acc))
        state, ss = new, nxt
        if g == a - 1 or a == 0:
            den0 = state[0][1][:, aux[0]:aux[0] + 1]
            den1 = state[1][1][:, aux[1]:aux[1] + 1]
            o_ref[0, rows_of(a), :] = jnp.where(head_w[0], state[0][1] / den0, state[1][1] / den1)


def moba_attention(att3):
    bsz, seq, _ = att3.shape
    nb = seq // MOBA_BLOCK
    assert nb % 2 == 0 and nb <= HEAD_DIM, "key blocks are visited in pairs and indexed on 64 spare lanes"
    hp = MOBA_WIDTH // LANES
    qkv = lambda part: pl.BlockSpec((1, seq, LANES), lambda b, p, part=part: (b, 0, part * hp + p))
    return pl.pallas_call(
        functools.partial(_moba_kernel, nb=nb),
        out_shape=jax.ShapeDtypeStruct((bsz, seq, MOBA_WIDTH), F32), grid=(bsz, hp),
        in_specs=[qkv(0), qkv(1), qkv(2)],
        out_specs=pl.BlockSpec((1, seq, LANES), lambda b, p: (b, 0, p)),
        scratch_shapes=[pltpu.VMEM((2, seq, LANES), BF16), pltpu.VMEM((2, seq, LANES), BF16),
                        pltpu.VMEM((2, seq, LANES), BF16), pltpu.VMEM((-(-nb // 8) * 8, LANES), F32)],
        compiler_params=_params("parallel", "parallel"), name="moba_attention",
    )(att3, att3, att3)


def _dilated_kernel(q_ref, k_ref, v_ref, o_ref, lse_ref, qd_ref, kd_ref, va_ref, *, seq, rate):
    qb = DIL_QBLOCK
    cpb = seq // rate // qb
    shift = cpb.bit_length() - 1
    lane, head, aux = _head_lanes((qb, LANES))

    def token_rows(n):
        if rate == 1:
            return pl.ds(pl.multiple_of(n * qb, qb), qb)
        c, ch = lax.shift_right_logical(n, shift), n & (cpb - 1)
        return pl.ds(c + ch * (qb * rate), qb, stride=rate)

    def prep(n):
        src = token_rows(n)
        dst = pl.ds(pl.multiple_of(n * qb, qb), qb)
        kn = k_ref[0, src, :]
        v = v_ref[0, src, :]
        qd_ref[dst, :] = (q_ref[0, src, :] * QSCALE).astype(BF16)
        for h in range(2):
            kd_ref[h, dst, :] = jnp.where(head[h], kn, 0.0).astype(BF16)
            va_ref[h, dst, :] = jnp.concatenate(
                [jnp.where(head[h], v, 0.0), jnp.where(head[h], 1.0, 0.0)], axis=1).astype(BF16)

    def unrolled(fn):
        def step(it, carry):
            for u in range(DIL_UNROLL):
                fn(it * DIL_UNROLL + u)
            return carry
        lax.fori_loop(0, seq // qb // DIL_UNROLL, step, 0)

    unrolled(prep)

    rel = (lax.broadcasted_iota(jnp.int32, (qb, 2 * qb), 0)
           - lax.broadcasted_iota(jnp.int32, (qb, 2 * qb), 1))
    nt = (((1,), (1,)), ((), ()))

    def qblock(n):
        first = (n & (cpb - 1)) == 0
        k0 = jnp.where(first, n, n - 1) * qb
        krows = pl.ds(pl.multiple_of(k0, qb), 2 * qb)
        qrows = pl.ds(pl.multiple_of(n * qb, qb), qb)
        dist = rel + (n * qb - k0)
        ok = (dist >= 0) & (dist <= DIL_WINDOW)
        keys = jnp.concatenate([kd_ref[0, krows, :], kd_ref[1, krows, :]], axis=0)
        s = lax.dot_general(qd_ref[qrows, :], keys, nt, preferred_element_type=F32)
        ms, ps = [], []
        for h in range(2):
            sh = jnp.where(ok, s[:, h * 2 * qb:(h + 1) * 2 * qb], NEG_INF)
            ms.append(jnp.max(sh, axis=-1, keepdims=True))
            ps.append(jnp.exp2(sh - ms[h]).astype(BF16))
        vals = jnp.concatenate([va_ref[0, krows, :], va_ref[1, krows, :]], axis=0)
        acc = jnp.dot(jnp.concatenate(ps, axis=1), vals, preferred_element_type=F32)
        den = acc[:, LANES:]
        dst = token_rows(n)
        o_ref[0, dst, :] = acc[:, :LANES] / den
        lse_ref[0, dst, :] = jnp.where(head[0], ms[0], ms[1]) + jnp.log2(den)

    unrolled(qblock)


def dilated_group(att3, group):
    bsz, seq, cols = att3.shape
    rate = DIL_RATES[group]
    cpb = seq // rate // DIL_QBLOCK
    assert cpb >= 2 and cpb & (cpb - 1) == 0 and (seq // DIL_QBLOCK) % DIL_UNROLL == 0
    hp = DIL_WIDTH // LANES
    base = COL_DIL // LANES
    qkv = lambda part: pl.BlockSpec(
        (1, seq, LANES), lambda b, p, part=part: (b, 0, base + (part * DIL_GROUPS + group) * hp + p))
    out = pl.BlockSpec((1, seq, LANES), lambda b, p: (b, 0, p))
    shp = jax.ShapeDtypeStruct((bsz, seq, DIL_WIDTH), F32)
    return pl.pallas_call(
        functools.partial(_dilated_kernel, seq=seq, rate=rate), out_shape=(shp, shp), grid=(bsz, hp),
        in_specs=[qkv(0), qkv(1), qkv(2)],
        out_specs=(out, out),
        scratch_shapes=[pltpu.VMEM((seq, LANES), BF16), pltpu.VMEM((2, seq, LANES), BF16),
                        pltpu.VMEM((2, seq, 2 * LANES), BF16)],
        compiler_params=_params("parallel", "parallel"), name=f"dilated_rate{rate}",
    )(att3, att3, att3)


def _ssd_kernel(z_ref, xs_ref, bc_ref, dt_ref, cwx_ref, cwb_ref, cbx_ref, cbb_ref, dtb_ref, alog_ref,
                dexp_ref, onorm_ref, tri_ref, triu_ref, exp_ref, o_ref, xpx_ref, xpb_ref, st_ref):
    L = SSM_CHUNK
    pad = 8

    @pl.when(pl.program_id(1) == 0)
    def _():
        xpx_ref[0:pad, :] = jnp.zeros((pad, SSM_INNER), F32)
        xpb_ref[0:pad, :] = jnp.zeros((pad, SSM_INNER), F32)
        st_ref[...] = jnp.zeros(st_ref.shape, F32)

    def conv_silu(src_ref, pad_ref, w_ref, b_ref):
        pad_ref[pad:, :] = src_ref[0].astype(F32)
        acc = b_ref[...] + w_ref[0:1, :] * pad_ref[pl.ds(pad - SSM_CONV + 1, L), :]
        for k in range(1, SSM_CONV):
            acc = acc + w_ref[k:k + 1, :] * pad_ref[pl.ds(pad - SSM_CONV + 1 + k, L), :]
        pad_ref[0:pad, :] = pad_ref[L:L + pad, :]
        return _silu(acc)

    xs = conv_silu(xs_ref, xpx_ref, cwx_ref, cbx_ref)
    bc = conv_silu(bc_ref, xpb_ref, cwb_ref, cbb_ref)
    gn = SSM_GROUPS * SSM_STATE
    bm, cm = bc[:, :gn], bc[:, gn:]

    xr = dt_ref[0] + dtb_ref[...]
    dt = jnp.maximum(xr, 0.0) + jnp.log(1.0 + jnp.exp(-jnp.abs(xr)))
    adt = dt * (-jnp.exp(alog_ref[...]))
    acs = _dot_exact_lhs(tri_ref[...], adt)
    acs_t = _dot_exact_rhs(adt.T, triu_ref[...])
    expand = exp_ref[...]
    dt_e = _dot_exact_rhs(dt, expand)
    acs_e = _dot_exact_rhs(acs, expand)
    xdt = xs * dt_e
    last = acs_e[L - 1:L, :]
    grow = jnp.exp(acs_e)
    to_end = jnp.exp(last - acs_e)
    chunk_decay = jnp.exp(last)

    ll = lax.broadcasted_iota(jnp.int32, (L, L), 0)
    ss = lax.broadcasted_iota(jnp.int32, (L, L), 1)
    causal = ll >= ss
    gw = SSM_INNER // SSM_GROUPS
    hpg = SSM_HEADS // SSM_GROUPS
    lane = lax.broadcasted_iota(jnp.int32, (L, gw), 1)
    nt = (((1,), (1,)), ((), ()))
    tn = (((0,), (0,)), ((), ()))
    ys = []
    for g in range(SSM_GROUPS):
        cols = slice(g * gw, (g + 1) * gw)
        bg = bm[:, g * SSM_STATE:(g + 1) * SSM_STATE].astype(BF16)
        cg = cm[:, g * SSM_STATE:(g + 1) * SSM_STATE].astype(BF16)
        xg = xdt[:, cols]
        cb = lax.dot_general(cg, bg, nt, preferred_element_type=F32)
        st = st_ref[g]
        y = jnp.dot(cg, st.astype(BF16), preferred_element_type=F32) * grow[:, cols]
        new = lax.dot_general(bg, (xg * to_end[:, cols]).astype(BF16), tn, preferred_element_type=F32)
        st_ref[g] = chunk_decay[:, cols] * st + new
        for hh in range(hpg):
            h = g * hpg + hh
            diff = acs[:, h:h + 1] - acs_t[h:h + 1, :]
            mat = (cb * jnp.exp(jnp.where(causal, diff, -jnp.inf))).astype(BF16)
            xh = jnp.where((lane >= hh * SSM_HEAD_DIM) & (lane < (hh + 1) * SSM_HEAD_DIM), xg, 0.0)
            y = y + jnp.dot(mat, xh.astype(BF16), preferred_element_type=F32)
        ys.append(y)
    y = jnp.concatenate(ys, axis=1) + xs * dexp_ref[...]
    yg = y * _silu(z_ref[0].astype(F32))
    ms = jnp.mean(yg * yg, axis=-1, keepdims=True)
    o_ref[0] = yg * lax.rsqrt(ms + NORM_EPS) * onorm_ref[...]


def ssd_mixer(mix3, dt3, conv_w, conv_b, dt_bias, a_log, d_exp, out_norm, consts):
    bsz, seq, _ = mix3.shape
    L = SSM_CHUNK
    w = SSM_INNER
    tri, triu, expand = consts
    col = lambda idx: pl.BlockSpec((1, L, w), lambda b, c, idx=idx: (b, c, idx))
    vecw = lambda rows, idx: pl.BlockSpec((rows, w), lambda b, c, idx=idx: (0, idx))
    vec = pl.BlockSpec((1, LANES), lambda b, c: (0, 0))
    sq = pl.BlockSpec((L, L), lambda b, c: (0, 0))
    return pl.pallas_call(
        _ssd_kernel, out_shape=jax.ShapeDtypeStruct((bsz, seq, w), F32), grid=(bsz, seq // L),
        in_specs=[col(MIX_Z // w), col(MIX_XBC // w), col(MIX_XBC // w + 1),
                  pl.BlockSpec((1, L, LANES), lambda b, c: (b, c, 0)),
                  vecw(SSM_CONV, 0), vecw(SSM_CONV, 1), vecw(1, 0), vecw(1, 1), vec, vec,
                  vecw(1, 0), vecw(1, 0), sq, sq, pl.BlockSpec((LANES, w), lambda b, c: (0, 0))],
        out_specs=pl.BlockSpec((1, L, w), lambda b, c: (b, c, 0)),
        scratch_shapes=[pltpu.VMEM((L + 8, w), F32), pltpu.VMEM((L + 8, w), F32),
                        pltpu.VMEM((SSM_GROUPS, SSM_STATE, w // SSM_GROUPS), F32)],
        compiler_params=_params("parallel", "arbitrary"), name="ssd_mixer",
    )(mix3, mix3, mix3, dt3, conv_w, conv_w, conv_b, conv_b, dt_bias, a_log, d_exp, out_norm,
      tri, triu, expand)


def _merge_kernel(x_ref, a_ref, m_ref, o0_ref, o1_ref, o2_ref, l0_ref, l1_ref, l2_ref, gl_ref, bg_ref,
                  wa_ref, wm_ref, wc_ref, wo_ref, out_ref):
    l0, l1, l2 = l0_ref[...], l1_ref[...], l2_ref[...]
    lmax = jnp.maximum(jnp.maximum(l0, l1), l2)
    e0, e1, e2 = jnp.exp2(l0 - lmax), jnp.exp2(l1 - lmax), jnp.exp2(l2 - lmax)
    cmix = (e0 * o0_ref[...] + e1 * o1_ref[...] + e2 * o2_ref[...]) / (e0 + e1 + e2)
    gates = _sigmoid(gl_ref[...] + bg_ref[...])
    d = D_MODEL
    mm = lambda v, w_ref: jnp.dot(v.astype(BF16), w_ref[...], preferred_element_type=F32)
    merged = (gates[:, :d] * mm(a_ref[...], wa_ref) + gates[:, d:2 * d] * mm(m_ref[...], wm_ref)
              + gates[:, 2 * d:] * mm(cmix, wc_ref))
    out_ref[...] = x_ref[...] + mm(merged, wo_ref)


def merge_branches(x2, a2, m2, dil, mix2, b_gate, wa, wm, wc, wo, tm=512):
    n, d = x2.shape
    tm = min(tm, n)
    row = lambda width, idx=0: pl.BlockSpec((tm, width), lambda i, idx=idx: (i, idx))
    full = lambda arr: pl.BlockSpec(arr.shape, lambda i: (0, 0), pipeline_mode=pl.Buffered(1))
    (o0, l0), (o1, l1), (o2, l2) = dil
    gw = N_BRANCH * d
    return pl.pallas_call(
        _merge_kernel, out_shape=jax.ShapeDtypeStruct((n, d), F32), grid=(n // tm,),
        in_specs=[row(d), row(MOBA_WIDTH), row(SSM_INNER)] + [row(DIL_WIDTH)] * 6
                 + [row(gw, MIX_GATE // gw), full(b_gate), full(wa), full(wm), full(wc), full(wo)],
        out_specs=row(d), compiler_params=_params("parallel"), name="merge_branches",
    )(x2, a2, m2, o0, o1, o2, l0, l1, l2, mix2, b_gate, wa, wm, wc, wo)


def _rms_bf16(x, gain):
    ms = jnp.mean(x * x, axis=-1, keepdims=True)
    return (x * lax.rsqrt(ms + NORM_EPS) * gain).astype(BF16)


def _ffn_ple_kernel(x_ref, xh_ref, p_ref, gf_ref, wup_ref, cw_ref, cb_ref, wd_ref, gp_ref, wg_ref, wp_ref,
                    o_ref, u_ref, buf_ref, h_ref, *, tm, tiles_per_seq):
    ck = FFN_CHUNK
    nck = FFN_DIM // ck
    u_ref[0:HALO, :] = _rms_bf16(xh_ref[...], gf_ref[...])
    u_ref[HALO:, :] = _rms_bf16(x_ref[...], gf_ref[...])
    keep = jnp.where(pl.program_id(0) % tiles_per_seq == 0, 0.0, 1.0)

    def up(c):
        u = u_ref[...]
        for half in range(2):
            buf = buf_ref.at[2 * (c % 2) + half]
            cols = slice(half * FFN_DIM + c * ck, half * FFN_DIM + (c + 1) * ck)
            val = jnp.dot(u, wup_ref[:, cols], preferred_element_type=F32)
            buf[0:HALO, :] = val[0:HALO, :] * keep
            buf[HALO:, :] = val[HALO:, :]

    def conv(c, half):
        buf = buf_ref.at[2 * (c % 2) + half]
        cols = slice(half * FFN_DIM + c * ck, half * FFN_DIM + (c + 1) * ck)
        acc = cb_ref[:, cols] + cw_ref[0:1, cols] * buf[pl.ds(HALO - FFN_CONV + 1, tm), :]
        for k in range(1, FFN_CONV):
            acc = acc + cw_ref[k:k + 1, cols] * buf[pl.ds(HALO - FFN_CONV + 1 + k, tm), :]
        return acc

    up(0)
    for c in range(nck):
        if c + 1 < nck:
            up(c + 1)
        h_ref[:, c * ck:(c + 1) * ck] = (_silu(conv(c, 0)) * conv(c, 1)).astype(BF16)
    acc = x_ref[...] + jnp.dot(h_ref[...], wd_ref[...], preferred_element_type=F32)
    pg = _sigmoid(jnp.dot(_rms_bf16(acc, gp_ref[...]), wg_ref[...], preferred_element_type=F32))
    o_ref[...] = acc + jnp.dot(p_ref[...].astype(BF16), wp_ref[...], preferred_element_type=F32) * pg


def ffn_ple(x2, p2, gain_ffn, w_up, conv_w, conv_b, w_down, gain_ple, w_gate, w_ple, seq, tm=512):
    n, d = x2.shape
    tm = min(tm, seq)
    hb = tm // HALO
    row = lambda width: pl.BlockSpec((tm, width), lambda i: (i, 0))
    resident = lambda arr: pl.BlockSpec(arr.shape, lambda i: (0, 0), pipeline_mode=pl.Buffered(1))
    return pl.pallas_call(
        functools.partial(_ffn_ple_kernel, tm=tm, tiles_per_seq=seq // tm),
        out_shape=jax.ShapeDtypeStruct((n, d), F32), grid=(n // tm,),
        in_specs=[row(d), pl.BlockSpec((HALO, d), lambda i: (jnp.maximum(i * hb - 1, 0), 0)), row(PLE_DIM),
                  resident(gain_ffn), resident(w_up), resident(conv_w), resident(conv_b), resident(w_down),
                  resident(gain_ple), resident(w_gate), resident(w_ple)],
        out_specs=row(d),
        scratch_shapes=[pltpu.VMEM((tm + HALO, d), BF16), pltpu.VMEM((4, tm + HALO, FFN_CHUNK), F32),
                        pltpu.VMEM((tm, FFN_DIM), BF16)],
        compiler_params=_params("parallel"), name="ffn_ple",
    )(x2, x2, p2, gain_ffn, w_up, conv_w, conv_b, w_down, gain_ple, w_gate, w_ple)


def _constants():
    lane = np.arange(LANES)
    bd = (lane[:, None] // HEAD_DIM == lane[None, :] // HEAD_DIM).astype(np.float32) / HEAD_DIM
    r = np.arange(SSM_CHUNK)
    tri = (r[None, :] <= r[:, None]).astype(np.float32)
    expand = (lane[:, None] == (np.arange(SSM_INNER)[None, :] // SSM_HEAD_DIM)).astype(np.float32)
    as_bf16 = lambda a: jnp.asarray(a, dtype=BF16)
    return as_bf16(bd), (as_bf16(tri), as_bf16(tri.T), as_bf16(expand))


def _pad_lanes(v):
    return jnp.pad(v, (0, LANES - v.shape[0]))[None, :]


def kernel(x, p, positions, norm_mix, w_in, b_gate, moba_q_norm, moba_k_norm, dil_q_norm, dil_k_norm,
           ssm_conv_w, ssm_conv_b, ssm_dt_bias, ssm_a_log, ssm_d, ssm_out_norm, w_br_moba, w_br_ssm,
           w_br_dil, w_out, norm_ffn, w_up, ffn_conv_w, ffn_conv_b, w_down, norm_ple, w_ple_gate, w_ple):
    bsz, seq, d = x.shape
    depth = w_in.shape[0]
    n = bsz * seq
    bd, ssd_consts = _constants()
    tables = [t.reshape(n, LANES) for t in rope_tables(positions)]
    row = lambda v: v[None, :]
    heads = lambda v, count: jnp.tile(v, count)
    ones = lambda count: jnp.ones((count,), F32)

    x2 = x.reshape(n, d)
    for i in range(depth):
        w_all = w_in[i].astype(BF16)
        w_dt = jnp.pad(w_all[:, COL_DT:COL_GATE], ((0, 0), (0, LANES - SSM_HEADS)))
        head_gain = jnp.concatenate([
            heads(moba_q_norm[i], MOBA_HEADS), heads(moba_k_norm[i], MOBA_HEADS), ones(MOBA_WIDTH),
            heads(dil_q_norm[i], DIL_GROUPS * DIL_HEADS), heads(dil_k_norm[i], DIL_GROUPS * DIL_HEADS),
            ones(DIL_GROUPS * DIL_WIDTH)])
        att2, mix2, dt2 = in_projection(x2, row(norm_mix[i]), w_all, w_all[:, COL_GATE:], w_dt,
                                        row(head_gain), bd, tables)
        att3 = att2.reshape(bsz, seq, COL_Z)

        out_a = moba_attention(att3)
        out_b = ssd_mixer(mix2.reshape(bsz, seq, MIX_COLS), dt2.reshape(bsz, seq, LANES), ssm_conv_w[i],
                          row(ssm_conv_b[i]), _pad_lanes(ssm_dt_bias[i]), _pad_lanes(ssm_a_log[i]),
                          row(jnp.repeat(ssm_d[i], SSM_HEAD_DIM)), row(ssm_out_norm[i]), ssd_consts)
        dil = [dilated_group(att3, g) for g in range(DIL_GROUPS)]
        dil2 = [(o.reshape(n, DIL_WIDTH), l.reshape(n, DIL_WIDTH)) for o, l in dil]

        x2 = merge_branches(x2, out_a.reshape(n, MOBA_WIDTH), out_b.reshape(n, SSM_INNER), dil2, mix2,
                            row(b_gate[i]), w_br_moba[i].astype(BF16), w_br_ssm[i].astype(BF16),
                            w_br_dil[i].astype(BF16), w_out[i].astype(BF16))
        x2 = ffn_ple(x2, p[i].reshape(n, PLE_DIM), row(norm_ffn[i]), w_up[i].astype(BF16), ffn_conv_w[i],
                     row(ffn_conv_b[i]), w_down[i].astype(BF16), row(norm_ple[i]),
                     w_ple_gate[i].astype(BF16), w_ple[i].astype(BF16), seq)
    return x2.reshape(bsz, seq, d)
```

```python
import functools
import math

import numpy as np
import jax
import jax.numpy as jnp
from jax import lax
from jax.experimental import pallas as pl
from jax.experimental.pallas import tpu as pltpu

F32 = jnp.float32
BF16 = jnp.bfloat16

D_MODEL = 1024
PLE_DIM = 256
HEAD_DIM = 64
ROPE_DIM = HEAD_DIM // 4
ROPE_THETA = 500000.0
NORM_EPS = 1e-6
NEG_INF = -1e30

MOBA_HEADS = 8
MOBA_BLOCK = 256
MOBA_TOPK = 3
MOBA_WIDTH = MOBA_HEADS * HEAD_DIM

DIL_RATES = (1, 4, 16)
DIL_GROUPS = 3
DIL_HEADS = 8
DIL_WINDOW = 128
DIL_WIDTH = DIL_HEADS * HEAD_DIM
DIL_QBLOCK = 128
DIL_UNROLL = 4

SSM_INNER = D_MODEL
SSM_HEAD_DIM = 64
SSM_HEADS = SSM_INNER // SSM_HEAD_DIM
SSM_GROUPS = 4
SSM_STATE = 128
SSM_CONV = 4
SSM_CHUNK = 128
SSM_XBC = SSM_INNER + 2 * SSM_GROUPS * SSM_STATE

FFN_DIM = 2816
FFN_CONV = 3
FFN_CHUNK = 256
N_BRANCH = 3

COL_MOBA = 0
COL_DIL = 3 * MOBA_WIDTH
COL_Z = COL_DIL + 3 * DIL_GROUPS * DIL_WIDTH
COL_XBC = COL_Z + SSM_INNER
COL_DT = COL_XBC + SSM_XBC
COL_GATE = COL_DT + SSM_HEADS
IN_COLS = COL_GATE + N_BRANCH * D_MODEL
IN_TILE = 1536
QK_ROWS = 256
ATT_TILES = COL_Z // IN_TILE
MAIN_TILES = COL_DT // IN_TILE
GATE_TILES = N_BRANCH * D_MODEL // IN_TILE
MIX_Z = 0
MIX_XBC = SSM_INNER
MIX_GATE = SSM_INNER + SSM_XBC
MIX_COLS = MIX_GATE + N_BRANCH * D_MODEL

QSCALE = HEAD_DIM ** -0.5 * math.log2(math.e)

LANES = 128
HALO = 16
VMEM_LIMIT = 56 * 1024 * 1024


def _params(*sem):
    return pltpu.CompilerParams(dimension_semantics=sem, vmem_limit_bytes=VMEM_LIMIT)


def _sigmoid(x):
    return 1.0 / (1.0 + jnp.exp2(x * -math.log2(math.e)))


def _silu(x):
    return x * _sigmoid(x)


def _split3(a):
    a1 = a.astype(BF16)
    r1 = a - a1.astype(F32)
    a2 = r1.astype(BF16)
    a3 = (r1 - a2.astype(F32)).astype(BF16)
    return a1, a2, a3


def _dot_exact_rhs(a, b_exact, passes=3):
    out = None
    for piece in _split3(a)[:passes]:
        t = jnp.dot(piece, b_exact, preferred_element_type=F32)
        out = t if out is None else out + t
    return out


def _dot_exact_lhs(a_exact, b, passes=3):
    out = None
    for piece in _split3(b)[:passes]:
        t = jnp.dot(a_exact, piece, preferred_element_type=F32)
        out = t if out is None else out + t
    return out


def _head_norm_rope(x, bd, gcos, gup, gdn):
    ms = jnp.dot((x * x).astype(BF16), bd, preferred_element_type=F32)
    half = ROPE_DIM // 2
    rot = x * gcos + pltpu.roll(x, half, 1) * gup + pltpu.roll(x, LANES - half, 1) * gdn
    return rot * lax.rsqrt(ms + NORM_EPS)


def _rope_kernel(pos_ref, inv_ref, cos_ref, up_ref, dn_ref):
    ang = pos_ref[0] * inv_ref[...]
    d = lax.broadcasted_iota(jnp.int32, ang.shape, 1) % HEAD_DIM
    half = ROPE_DIM // 2
    s = jnp.sin(ang)
    cos_ref[0] = jnp.cos(ang)
    up_ref[0] = jnp.where((d >= half) & (d < ROPE_DIM), s, 0.0)
    dn_ref[0] = jnp.where(d < half, -s, 0.0)


def rope_tables(positions):
    bsz, seq = positions.shape
    ts = min(seq, 1024)
    d = np.arange(LANES) % HEAD_DIM
    inv = ROPE_THETA ** (-jnp.arange(0, ROPE_DIM, 2, dtype=F32) / ROPE_DIM)
    inv_lane = jnp.where(d < ROPE_DIM, inv[d % (ROPE_DIM // 2)], 0.0).astype(F32)[None, :]
    pos = positions.astype(F32)[..., None]
    shp = jax.ShapeDtypeStruct((bsz, seq, LANES), F32)
    spec = pl.BlockSpec((1, ts, LANES), lambda b, t: (b, t, 0))
    return pl.pallas_call(
        _rope_kernel, out_shape=(shp, shp, shp), grid=(bsz, seq // ts),
        in_specs=[pl.BlockSpec((1, ts, 1), lambda b, t: (b, t, 0)),
                  pl.BlockSpec((1, LANES), lambda b, t: (0, 0))],
        out_specs=(spec, spec, spec), compiler_params=_params("parallel", "parallel"),
        name="rope_tables")(pos, inv_lane)


def _inproj_kernel(x_ref, g_ref, w_ref, wg_ref, wdt_ref, hg_ref, bd_ref, cos_ref, up_ref, dn_ref,
                   att_ref, mix_ref, dt_ref, u_ref, *, tm):
    j = pl.program_id(1)

    @pl.when(j == 0)
    def _():
        u = _rms_bf16(x_ref[...], g_ref[...])
        u_ref[...] = u
        dt_ref[...] = jnp.dot(u, wdt_ref[...], preferred_element_type=F32)

    def attention_tile(gain_sets):
        att_ref[...] = jnp.dot(u_ref[...], w_ref[...], preferred_element_type=F32)
        bd = bd_ref[...]
        for r0 in range(0, tm, QK_ROWS):
            rows = slice(r0, r0 + QK_ROWS)
            for first, count in gain_sets:
                g0 = slice(first * LANES, (first + 1) * LANES)
                gcos = hg_ref[0:1, g0] * cos_ref[rows, :]
                gup = hg_ref[1:2, g0] * up_ref[rows, :]
                gdn = hg_ref[2:3, g0] * dn_ref[rows, :]
                for g in range(first, first + count):
                    lanes = slice(g * LANES, (g + 1) * LANES)
                    att_ref[rows, lanes] = _head_norm_rope(att_ref[rows, lanes], bd, gcos, gup, gdn)

    per_part = MOBA_WIDTH // LANES

    @pl.when(j == 0)
    def _():
        attention_tile([(0, per_part), (per_part, per_part)])

    @pl.when((j == 1) | (j == 2))
    def _():
        attention_tile([(0, IN_TILE // LANES)])

    @pl.when(j == 3)
    def _():
        attention_tile([])

    @pl.when((j >= ATT_TILES) & (j < MAIN_TILES))
    def _():
        mix_ref[...] = jnp.dot(u_ref[...], w_ref[...], preferred_element_type=F32).astype(BF16)

    @pl.when(j >= MAIN_TILES)
    def _():
        mix_ref[...] = jnp.dot(u_ref[...], wg_ref[...], preferred_element_type=F32).astype(BF16)


def in_projection(x2, gain, w_all, w_gate, w_dt, head_gain, bd, tables, tm=1024):
    n, d = x2.shape
    tm = min(tm, n)
    tn = IN_TILE
    assert 3 * MOBA_WIDTH == tn and DIL_GROUPS * DIL_WIDTH == tn and tm % QK_ROWS == 0
    tab = pl.BlockSpec((tm, LANES), lambda i, j: (i, 0))
    return pl.pallas_call(
        functools.partial(_inproj_kernel, tm=tm),
        out_shape=(jax.ShapeDtypeStruct((n, COL_Z), F32), jax.ShapeDtypeStruct((n, MIX_COLS), BF16),
                   jax.ShapeDtypeStruct((n, LANES), F32)),
        grid=(n // tm, MAIN_TILES + GATE_TILES),
        in_specs=[pl.BlockSpec((tm, d), lambda i, j: (i, 0)),
                  pl.BlockSpec((1, d), lambda i, j: (0, 0)),
                  pl.BlockSpec((d, tn), lambda i, j: (0, jnp.minimum(j, MAIN_TILES - 1))),
                  pl.BlockSpec((d, tn), lambda i, j: (0, jnp.maximum(j - MAIN_TILES, 0))),
                  pl.BlockSpec((d, LANES), lambda i, j: (0, 0)),
                  pl.BlockSpec((3, tn), lambda i, j: (0, jnp.minimum(j, ATT_TILES - 1))),
                  pl.BlockSpec((LANES, LANES), lambda i, j: (0, 0)), tab, tab, tab],
        out_specs=(pl.BlockSpec((tm, tn), lambda i, j: (i, jnp.minimum(j, ATT_TILES - 1))),
                   pl.BlockSpec((tm, tn), lambda i, j: (i, jnp.maximum(j - ATT_TILES, 0))),
                   pl.BlockSpec((tm, LANES), lambda i, j: (i, 0))),
        scratch_shapes=[pltpu.VMEM((tm, d), BF16)],
        compiler_params=_params("parallel", "arbitrary"), name="in_projection",
    )(x2, gain, w_all, w_gate, w_dt, head_gain, bd, *tables)


def _head_lanes(shape):
    lane = lax.broadcasted_iota(jnp.int32, shape, len(shape) - 1)
    return lane, (lane < HEAD_DIM, lane >= HEAD_DIM), (HEAD_DIM, 0)


def _moba_kernel(q_ref, k_ref, v_ref, o_ref, qa_ref, ka_ref, va_ref, km_ref, *, nb):
    blk = MOBA_BLOCK
    lane, head, aux = _head_lanes((blk, LANES))
    nbp = km_ref.shape[0]
    km_ref[...] = jnp.zeros(km_ref.shape, F32)

    def prep(j):
        rows = pl.ds(pl.multiple_of(j * blk, blk), blk)
        kn = k_ref[0, rows, :]
        v = v_ref[0, rows, :]
        km_ref[pl.ds(j, 1), :] = jnp.mean(kn, axis=0, keepdims=True)
        for h in range(2):
            ka_ref[h, rows, :] = jnp.where(head[h], kn, jnp.where(lane == aux[h] + j, 1.0, 0.0)).astype(BF16)
            va_ref[h, rows, :] = jnp.where(head[h], v, jnp.where(lane == aux[h], 1.0, 0.0)).astype(BF16)

    def prep2(t, carry):
        prep(2 * t)
        prep(2 * t + 1)
        return carry

    lax.fori_loop(0, nb // 2, prep2, 0)

    bidx = lax.broadcasted_iota(jnp.int32, (nbp, blk), 0).astype(F32)
    isblk = bidx < float(nb)
    _, head_k, _ = _head_lanes((nbp, LANES))
    nt = (((1,), (1,)), ((), ()))

    def select(i):
        rows = pl.ds(pl.multiple_of(i * blk, blk), blk)
        qf = q_ref[0, rows, :]
        q_hi = qf.astype(BF16)
        q_lo = (qf - q_hi.astype(F32)).astype(BF16)
        i_f = lax.convert_element_type(i, F32)
        for h in range(2):
            km = jnp.where(head_k[h], km_ref[...], 0.0)
            k_hi = km.astype(BF16)
            k_lo = (km - k_hi.astype(F32)).astype(BF16)
            dot = lambda a, b: lax.dot_general(a, b, nt, preferred_element_type=F32)
            sc = dot(k_hi, q_hi) + (dot(k_hi, q_lo) + dot(k_lo, q_hi))
            valid = isblk & (bidx < i_f)
            cur = jnp.where(valid, sc, -jnp.inf)
            sel = jnp.zeros((nbp, blk), F32)
            for _ in range(min(MOBA_TOPK, nb)):
                mx = jnp.max(cur, axis=0, keepdims=True)
                first = jnp.min(jnp.where((cur == mx) & isblk, bidx, float(nbp)), axis=0, keepdims=True)
                hit = bidx == first
                sel = jnp.where(hit, 1.0, sel)
                cur = jnp.where(hit, -jnp.inf, cur)
            keep = ((sel > 0.5) & valid) | (bidx == i_f)
            bias = jnp.where(isblk & jnp.logical_not(keep), NEG_INF, 0.0)
            pieces = [jnp.zeros((aux[h], blk), F32)] if aux[h] else []
            pieces += [bias, jnp.zeros((LANES - aux[h] - nbp, blk), F32)]
            bias_t = jnp.concatenate(pieces, axis=0).T
            qa_ref[h, rows, :] = jnp.where(head[h], qf * QSCALE, bias_t).astype(BF16)

    def select2(t, carry):
        select(2 * t)
        select(2 * t + 1)
        return carry

    lax.fori_loop(0, nb // 2, select2, 0)

    wide = 2 * blk
    _, head_w, _ = _head_lanes((wide, LANES))
    causal = (lax.broadcasted_iota(jnp.int32, (wide, wide), 1)
              <= lax.broadcasted_iota(jnp.int32, (wide, wide), 0))
    nt = (((1,), (1,)), ((), ()))

    def rows_of(t):
        return slice(t * wide, (t + 1) * wide)

    def logits(a, g):
        return [lax.dot_general(qa_ref[h, rows_of(a), :], ka_ref[h, rows_of(g), :], nt,
                                preferred_element_type=F32) for h in range(2)]

    tiles = [(a, g) for a in range(nb // 2) for g in [a] + list(range(a))]
    ss = logits(*tiles[0])
    state = None
    for t, (a, g) in enumerate(tiles):
        nxt = logits(*tiles[t + 1]) if t + 1 < len(tiles) else None
        new = []
        for h in range(2):
            s = ss[h]
            if g == a:
                s = jnp.where(causal, s, NEG_INF)
                m = jnp.max(s, axis=-1, keepdims=True)
                acc = jnp.dot(jnp.exp2(s - m).astype(BF16), va_ref[h, rows_of(g), :],
                              preferred_element_type=F32)
            else:
                m_old, acc_old = state[h]
                m = jnp.maximum(m_old, jnp.max(s, axis=-1, keepdims=True))
                acc = jnp.exp2(m_old - m) * acc_old + jnp.dot(
                    jnp.exp2(s - m).astype(BF16), va_ref[h, rows_of(g), :], preferred_element_type=F32)
            new.append((m, acc))
        state, ss = new, nxt
        if g == a - 1 or a == 0:
            den0 = state[0][1][:, aux[0]:aux[0] + 1]
            den1 = state[1][1][:, aux[1]:aux[1] + 1]
            o_ref[0, rows_of(a), :] = jnp.where(head_w[0], state[0][1] / den0, state[1][1] / den1)


def moba_attention(att3):
    bsz, seq, _ = att3.shape
    nb = seq // MOBA_BLOCK
    assert nb % 2 == 0 and nb <= HEAD_DIM, "key blocks are visited in pairs and indexed on 64 spare lanes"
    hp = MOBA_WIDTH // LANES
    qkv = lambda part: pl.BlockSpec((1, seq, LANES), lambda b, p, part=part: (b, 0, part * hp + p))
    return pl.pallas_call(
        functools.partial(_moba_kernel, nb=nb),
        out_shape=jax.ShapeDtypeStruct((bsz, seq, MOBA_WIDTH), F32), grid=(bsz, hp),
        in_specs=[qkv(0), qkv(1), qkv(2)],
        out_specs=pl.BlockSpec((1, seq, LANES), lambda b, p: (b, 0, p)),
        scratch_shapes=[pltpu.VMEM((2, seq, LANES), BF16), pltpu.VMEM((2, seq, LANES), BF16),
                        pltpu.VMEM((2, seq, LANES), BF16), pltpu.VMEM((-(-nb // 8) * 8, LANES), F32)],
        compiler_params=_params("parallel", "parallel"), name="moba_attention",
    )(att3, att3, att3)


def _dilated_kernel(q_ref, k_ref, v_ref, o_ref, lse_ref, qd_ref, kd_ref, va_ref, *, seq, rate):
    qb = DIL_QBLOCK
    cpb = seq // rate // qb
    shift = cpb.bit_length() - 1
    lane, head, aux = _head_lanes((qb, LANES))

    def token_rows(n):
        if rate == 1:
            return pl.ds(pl.multiple_of(n * qb, qb), qb)
        c, ch = lax.shift_right_logical(n, shift), n & (cpb - 1)
        return pl.ds(c + ch * (qb * rate), qb, stride=rate)

    def prep(n):
        src = token_rows(n)
        dst = pl.ds(pl.multiple_of(n * qb, qb), qb)
        kn = k_ref[0, src, :]
        v = v_ref[0, src, :]
        qd_ref[dst, :] = (q_ref[0, src, :] * QSCALE).astype(BF16)
        for h in range(2):
            kd_ref[h, dst, :] = jnp.where(head[h], kn, 0.0).astype(BF16)
            va_ref[h, dst, :] = jnp.concatenate(
                [jnp.where(head[h], v, 0.0), jnp.where(head[h], 1.0, 0.0)], axis=1).astype(BF16)

    def unrolled(fn):
        def step(it, carry):
            for u in range(DIL_UNROLL):
                fn(it * DIL_UNROLL + u)
            return carry
        lax.fori_loop(0, seq // qb // DIL_UNROLL, step, 0)

    unrolled(prep)

    rel = (lax.broadcasted_iota(jnp.int32, (qb, 2 * qb), 0)
           - lax.broadcasted_iota(jnp.int32, (qb, 2 * qb), 1))
    nt = (((1,), (1,)), ((), ()))

    def qblock(n):
        first = (n & (cpb - 1)) == 0
        k0 = jnp.where(first, n, n - 1) * qb
        krows = pl.ds(pl.multiple_of(k0, qb), 2 * qb)
        qrows = pl.ds(pl.multiple_of(n * qb, qb), qb)
        dist = rel + (n * qb - k0)
        ok = (dist >= 0) & (dist <= DIL_WINDOW)
        keys = jnp.concatenate([kd_ref[0, krows, :], kd_ref[1, krows, :]], axis=0)
        s = lax.dot_general(qd_ref[qrows, :], keys, nt, preferred_element_type=F32)
        ms, ps = [], []
        for h in range(2):
            sh = jnp.where(ok, s[:, h * 2 * qb:(h + 1) * 2 * qb], NEG_INF)
            ms.append(jnp.max(sh, axis=-1, keepdims=True))
            ps.append(jnp.exp2(sh - ms[h]).astype(BF16))
        vals = jnp.concatenate([va_ref[0, krows, :], va_ref[1, krows, :]], axis=0)
        acc = jnp.dot(jnp.concatenate(ps, axis=1), vals, preferred_element_type=F32)
        den = acc[:, LANES:]
        dst = token_rows(n)
        o_ref[0, dst, :] = acc[:, :LANES] / den
        lse_ref[0, dst, :] = jnp.where(head[0], ms[0], ms[1]) + jnp.log2(den)

    unrolled(qblock)


def dilated_group(att3, group):
    bsz, seq, cols = att3.shape
    rate = DIL_RATES[group]
    cpb = seq // rate // DIL_QBLOCK
    assert cpb >= 2 and cpb & (cpb - 1) == 0 and (seq // DIL_QBLOCK) % DIL_UNROLL == 0
    hp = DIL_WIDTH // LANES
    base = COL_DIL // LANES
    qkv = lambda part: pl.BlockSpec(
        (1, seq, LANES), lambda b, p, part=part: (b, 0, base + (part * DIL_GROUPS + group) * hp + p))
    out = pl.BlockSpec((1, seq, LANES), lambda b, p: (b, 0, p))
    shp = jax.ShapeDtypeStruct((bsz, seq, DIL_WIDTH), F32)
    return pl.pallas_call(
        functools.partial(_dilated_kernel, seq=seq, rate=rate), out_shape=(shp, shp), grid=(bsz, hp),
        in_specs=[qkv(0), qkv(1), qkv(2)],
        out_specs=(out, out),
        scratch_shapes=[pltpu.VMEM((seq, LANES), BF16), pltpu.VMEM((2, seq, LANES), BF16),
                        pltpu.VMEM((2, seq, 2 * LANES), BF16)],
        compiler_params=_params("parallel", "parallel"), name=f"dilated_rate{rate}",
    )(att3, att3, att3)


def _ssd_kernel(z_ref, xs_ref, bc_ref, dt_ref, cwx_ref, cwb_ref, cbx_ref, cbb_ref, dtb_ref, alog_ref,
                dexp_ref, onorm_ref, tri_ref, triu_ref, exp_ref, o_ref, xpx_ref, xpb_ref, st_ref):
    L = SSM_CHUNK
    pad = 8

    @pl.when(pl.program_id(1) == 0)
    def _():
        xpx_ref[0:pad, :] = jnp.zeros((pad, SSM_INNER), F32)
        xpb_ref[0:pad, :] = jnp.zeros((pad, SSM_INNER), F32)
        st_ref[...] = jnp.zeros(st_ref.shape, F32)

    def conv_silu(src_ref, pad_ref, w_ref, b_ref):
        pad_ref[pad:, :] = src_ref[0].astype(F32)
        acc = b_ref[...] + w_ref[0:1, :] * pad_ref[pl.ds(pad - SSM_CONV + 1, L), :]
        for k in range(1, SSM_CONV):
            acc = acc + w_ref[k:k + 1, :] * pad_ref[pl.ds(pad - SSM_CONV + 1 + k, L), :]
        pad_ref[0:pad, :] = pad_ref[L:L + pad, :]
        return _silu(acc)

    xs = conv_silu(xs_ref, xpx_ref, cwx_ref, cbx_ref)
    bc = conv_silu(bc_ref, xpb_ref, cwb_ref, cbb_ref)
    gn = SSM_GROUPS * SSM_STATE
    bm, cm = bc[:, :gn], bc[:, gn:]

    xr = dt_ref[0] + dtb_ref[...]
    dt = jnp.maximum(xr, 0.0) + jnp.log(1.0 + jnp.exp(-jnp.abs(xr)))
    adt = dt * (-jnp.exp(alog_ref[...]))
    acs = _dot_exact_lhs(tri_ref[...], adt)
    acs_t = _dot_exact_rhs(adt.T, triu_ref[...])
    expand = exp_ref[...]
    dt_e = _dot_exact_rhs(dt, expand)
    acs_e = _dot_exact_rhs(acs, expand)
    xdt = xs * dt_e
    last = acs_e[L - 1:L, :]
    grow = jnp.exp(acs_e)
    to_end = jnp.exp(last - acs_e)
    chunk_decay = jnp.exp(last)

    ll = lax.broadcasted_iota(jnp.int32, (L, L), 0)
    ss = lax.broadcasted_iota(jnp.int32, (L, L), 1)
    causal = ll >= ss
    gw = SSM_INNER // SSM_GROUPS
    hpg = SSM_HEADS // SSM_GROUPS
    lane = lax.broadcasted_iota(jnp.int32, (L, gw), 1)
    nt = (((1,), (1,)), ((), ()))
    tn = (((0,), (0,)), ((), ()))
    ys = []
    for g in range(SSM_GROUPS):
        cols = slice(g * gw, (g + 1) * gw)
        bg = bm[:, g * SSM_STATE:(g + 1) * SSM_STATE].astype(BF16)
        cg = cm[:, g * SSM_STATE:(g + 1) * SSM_STATE].astype(BF16)
        xg = xdt[:, cols]
        cb = lax.dot_general(cg, bg, nt, preferred_element_type=F32)
        st = st_ref[g]
        y = jnp.dot(cg, st.astype(BF16), preferred_element_type=F32) * grow[:, cols]
        new = lax.dot_general(bg, (xg * to_end[:, cols]).astype(BF16), tn, preferred_element_type=F32)
        st_ref[g] = chunk_decay[:, cols] * st + new
        for hh in range(hpg):
            h = g * hpg + hh
            diff = acs[:, h:h + 1] - acs_t[h:h + 1, :]
            mat = (cb * jnp.exp(jnp.where(causal, diff, -jnp.inf))).astype(BF16)
            xh = jnp.where((lane >= hh * SSM_HEAD_DIM) & (lane < (hh + 1) * SSM_HEAD_DIM), xg, 0.0)
            y = y + jnp.dot(mat, xh.astype(BF16), preferred_element_type=F32)
        ys.append(y)
    y = jnp.concatenate(ys, axis=1) + xs * dexp_ref[...]
    yg = y * _silu(z_ref[0].astype(F32))
    ms = jnp.mean(yg * yg, axis=-1, keepdims=True)
    o_ref[0] = yg * lax.rsqrt(ms + NORM_EPS) * onorm_ref[...]


def ssd_mixer(mix3, dt3, conv_w, conv_b, dt_bias, a_log, d_exp, out_norm, consts):
    bsz, seq, _ = mix3.shape
    L = SSM_CHUNK
    w = SSM_INNER
    tri, triu, expand = consts
    col = lambda idx: pl.BlockSpec((1, L, w), lambda b, c, idx=idx: (b, c, idx))
    vecw = lambda rows, idx: pl.BlockSpec((rows, w), lambda b, c, idx=idx: (0, idx))
    vec = pl.BlockSpec((1, LANES), lambda b, c: (0, 0))
    sq = pl.BlockSpec((L, L), lambda b, c: (0, 0))
    return pl.pallas_call(
        _ssd_kernel, out_shape=jax.ShapeDtypeStruct((bsz, seq, w), F32), grid=(bsz, seq // L),
        in_specs=[col(MIX_Z // w), col(MIX_XBC // w), col(MIX_XBC // w + 1),
                  pl.BlockSpec((1, L, LANES), lambda b, c: (b, c, 0)),
                  vecw(SSM_CONV, 0), vecw(SSM_CONV, 1), vecw(1, 0), vecw(1, 1), vec, vec,
                  vecw(1, 0), vecw(1, 0), sq, sq, pl.BlockSpec((LANES, w), lambda b, c: (0, 0))],
        out_specs=pl.BlockSpec((1, L, w), lambda b, c: (b, c, 0)),
        scratch_shapes=[pltpu.VMEM((L + 8, w), F32), pltpu.VMEM((L + 8, w), F32),
                        pltpu.VMEM((SSM_GROUPS, SSM_STATE, w // SSM_GROUPS), F32)],
        compiler_params=_params("parallel", "arbitrary"), name="ssd_mixer",
    )(mix3, mix3, mix3, dt3, conv_w, conv_w, conv_b, conv_b, dt_bias, a_log, d_exp, out_norm,
      tri, triu, expand)


def _merge_kernel(x_ref, a_ref, m_ref, o0_ref, o1_ref, o2_ref, l0_ref, l1_ref, l2_ref, gl_ref, bg_ref,
                  wa_ref, wm_ref, wc_ref, wo_ref, out_ref):
    l0, l1, l2 = l0_ref[...], l1_ref[...], l2_ref[...]
    lmax = jnp.maximum(jnp.maximum(l0, l1), l2)
    e0, e1, e2 = jnp.exp2(l0 - lmax), jnp.exp2(l1 - lmax), jnp.exp2(l2 - lmax)
    cmix = (e0 * o0_ref[...] + e1 * o1_ref[...] + e2 * o2_ref[...]) / (e0 + e1 + e2)
    gates = _sigmoid(gl_ref[...] + bg_ref[...])
    d = D_MODEL
    mm = lambda v, w_ref: jnp.dot(v.astype(BF16), w_ref[...], preferred_element_type=F32)
    merged = (gates[:, :d] * mm(a_ref[...], wa_ref) + gates[:, d:2 * d] * mm(m_ref[...], wm_ref)
              + gates[:, 2 * d:] * mm(cmix, wc_ref))
    out_ref[...] = x_ref[...] + mm(merged, wo_ref)


def merge_branches(x2, a2, m2, dil, mix2, b_gate, wa, wm, wc, wo, tm=512):
    n, d = x2.shape
    tm = min(tm, n)
    row = lambda width, idx=0: pl.BlockSpec((tm, width), lambda i, idx=idx: (i, idx))
    full = lambda arr: pl.BlockSpec(arr.shape, lambda i: (0, 0), pipeline_mode=pl.Buffered(1))
    (o0, l0), (o1, l1), (o2, l2) = dil
    gw = N_BRANCH * d
    return pl.pallas_call(
        _merge_kernel, out_shape=jax.ShapeDtypeStruct((n, d), F32), grid=(n // tm,),
        in_specs=[row(d), row(MOBA_WIDTH), row(SSM_INNER)] + [row(DIL_WIDTH)] * 6
                 + [row(gw, MIX_GATE // gw), full(b_gate), full(wa), full(wm), full(wc), full(wo)],
        out_specs=row(d), compiler_params=_params("parallel"), name="merge_branches",
    )(x2, a2, m2, o0, o1, o2, l0, l1, l2, mix2, b_gate, wa, wm, wc, wo)


def _rms_bf16(x, gain):
    ms = jnp.mean(x * x, axis=-1, keepdims=True)
    return (x * lax.rsqrt(ms + NORM_EPS) * gain).astype(BF16)


def _ffn_ple_kernel(x_ref, xh_ref, p_ref, gf_ref, wup_ref, cw_ref, cb_ref, wd_ref, gp_ref, wg_ref, wp_ref,
                    o_ref, u_ref, buf_ref, h_ref, *, tm, tiles_per_seq):
    ck = FFN_CHUNK
    nck = FFN_DIM // ck
    u_ref[0:HALO, :] = _rms_bf16(xh_ref[...], gf_ref[...])
    u_ref[HALO:, :] = _rms_bf16(x_ref[...], gf_ref[...])
    keep = jnp.where(pl.program_id(0) % tiles_per_seq == 0, 0.0, 1.0)

    def up(c):
        u = u_ref[...]
        for half in range(2):
            buf = buf_ref.at[2 * (c % 2) + half]
            cols = slice(half * FFN_DIM + c * ck, half * FFN_DIM + (c + 1) * ck)
            val = jnp.dot(u, wup_ref[:, cols], preferred_element_type=F32)
            buf[0:HALO, :] = val[0:HALO, :] * keep
            buf[HALO:, :] = val[HALO:, :]

    def conv(c, half):
        buf = buf_ref.at[2 * (c % 2) + half]
        cols = slice(half * FFN_DIM + c * ck, half * FFN_DIM + (c + 1) * ck)
        acc = cb_ref[:, cols] + cw_ref[0:1, cols] * buf[pl.ds(HALO - FFN_CONV + 1, tm), :]
        for k in range(1, FFN_CONV):
            acc = acc + cw_ref[k:k + 1, cols] * buf[pl.ds(HALO - FFN_CONV + 1 + k, tm), :]
        return acc

    up(0)
    for c in range(nck):
        if c + 1 < nck:
            up(c + 1)
        h_ref[:, c * ck:(c + 1) * ck] = (_silu(conv(c, 0)) * conv(c, 1)).astype(BF16)
    acc = x_ref[...] + jnp.dot(h_ref[...], wd_ref[...], preferred_element_type=F32)
    pg = _sigmoid(jnp.dot(_rms_bf16(acc, gp_ref[...]), wg_ref[...], preferred_element_type=F32))
    o_ref[...] = acc + jnp.dot(p_ref[...].astype(BF16), wp_ref[...], preferred_element_type=F32) * pg


def ffn_ple(x2, p2, gain_ffn, w_up, conv_w, conv_b, w_down, gain_ple, w_gate, w_ple, seq, tm=512):
    n, d = x2.shape
    tm = min(tm, seq)
    hb = tm // HALO
    row = lambda width: pl.BlockSpec((tm, width), lambda i: (i, 0))
    resident = lambda arr: pl.BlockSpec(arr.shape, lambda i: (0, 0), pipeline_mode=pl.Buffered(1))
    return pl.pallas_call(
        functools.partial(_ffn_ple_kernel, tm=tm, tiles_per_seq=seq // tm),
        out_shape=jax.ShapeDtypeStruct((n, d), F32), grid=(n // tm,),
        in_specs=[row(d), pl.BlockSpec((HALO, d), lambda i: (jnp.maximum(i * hb - 1, 0), 0)), row(PLE_DIM),
                  resident(gain_ffn), resident(w_up), resident(conv_w), resident(conv_b), resident(w_down),
                  resident(gain_ple), resident(w_gate), resident(w_ple)],
        out_specs=row(d),
        scratch_shapes=[pltpu.VMEM((tm + HALO, d), BF16), pltpu.VMEM((4, tm + HALO, FFN_CHUNK), F32),
                        pltpu.VMEM((tm, FFN_DIM), BF16)],
        compiler_params=_params("parallel"), name="ffn_ple",
    )(x2, x2, p2, gain_ffn, w_up, conv_w, conv_b, w_down, gain_ple, w_gate, w_ple)


def _constants():
    lane = np.arange(LANES)
    bd = (lane[:, None] // HEAD_DIM == lane[None, :] // HEAD_DIM).astype(np.float32) / HEAD_DIM
    r = np.arange(SSM_CHUNK)
    tri = (r[None, :] <= r[:, None]).astype(np.float32)
    expand = (lane[:, None] == (np.arange(SSM_INNER)[None, :] // SSM_HEAD_DIM)).astype(np.float32)
    as_bf16 = lambda a: jnp.asarray(a, dtype=BF16)
    return as_bf16(bd), (as_bf16(tri), as_bf16(tri.T), as_bf16(expand))


def _pad_lanes(v):
    return jnp.pad(v, (0, LANES - v.shape[0]))[None, :]


def kernel(x, p, positions, norm_mix, w_in, b_gate, moba_q_norm, moba_k_norm, dil_q_norm, dil_k_norm,
           ssm_conv_w, ssm_conv_b, ssm_dt_bias, ssm_a_log, ssm_d, ssm_out_norm, w_br_moba, w_br_ssm,
           w_br_dil, w_out, norm_ffn, w_up, ffn_conv_w, ffn_conv_b, w_down, norm_ple, w_ple_gate, w_ple):
    bsz, seq, d = x.shape
    depth = w_in.shape[0]
    n = bsz * seq
    bd, ssd_consts = _constants()
    tables = [t.reshape(n, LANES) for t in rope_tables(positions)]
    row = lambda v: v[None, :]
    heads = lambda v, count: jnp.tile(v, count)
    ones = lambda count: jnp.ones((count,), F32)

    x2 = x.reshape(n, d)
    for i in range(depth):
        w_all = w_in[i].astype(BF16)
        w_dt = jnp.pad(w_all[:, COL_DT:COL_GATE], ((0, 0), (0, LANES - SSM_HEADS)))
        half = ROPE_DIM // 2
        head_gain = jnp.stack([jnp.concatenate([
            heads(jnp.roll(moba_q_norm[i], s), MOBA_HEADS), heads(jnp.roll(moba_k_norm[i], s), MOBA_HEADS),
            ones(MOBA_WIDTH), heads(jnp.roll(dil_q_norm[i], s), DIL_GROUPS * DIL_HEADS),
            heads(jnp.roll(dil_k_norm[i], s), DIL_GROUPS * DIL_HEADS), ones(DIL_GROUPS * DIL_WIDTH)])
            for s in (0, half, -half)])
        att2, mix2, dt2 = in_projection(x2, row(norm_mix[i]), w_all, w_all[:, COL_GATE:], w_dt,
                                        head_gain, bd, tables)
        att3 = att2.reshape(bsz, seq, COL_Z)

        out_a = moba_attention(att3)
        out_b = ssd_mixer(mix2.reshape(bsz, seq, MIX_COLS), dt2.reshape(bsz, seq, LANES), ssm_conv_w[i],
                          row(ssm_conv_b[i]), _pad_lanes(ssm_dt_bias[i]), _pad_lanes(ssm_a_log[i]),
                          row(jnp.repeat(ssm_d[i], SSM_HEAD_DIM)), row(ssm_out_norm[i]), ssd_consts)
        dil = [dilated_group(att3, g) for g in range(DIL_GROUPS)]
        dil2 = [(o.reshape(n, DIL_WIDTH), l.reshape(n, DIL_WIDTH)) for o, l in dil]

        x2 = merge_branches(x2, out_a.reshape(n, MOBA_WIDTH), out_b.reshape(n, SSM_INNER), dil2, mix2,
                            row(b_gate[i]), w_br_moba[i].astype(BF16), w_br_ssm[i].astype(BF16),
                            w_br_dil[i].astype(BF16), w_out[i].astype(BF16))
        x2 = ffn_ple(x2, p[i].reshape(n, PLE_DIM), row(norm_ffn[i]), w_up[i].astype(BF16), ffn_conv_w[i],
                     row(ffn_conv_b[i]), w_down[i].astype(BF16), row(norm_ple[i]),
                     w_ple_gate[i].astype(BF16), w_ple[i].astype(BF16), seq)
    return x2.reshape(bsz, seq, d)
```

```python
import functools
import math

import numpy as np
import jax
import jax.numpy as jnp
from jax import lax
from jax.experimental import pallas as pl
from jax.experimental.pallas import tpu as pltpu

F32 = jnp.float32
BF16 = jnp.bfloat16

D_MODEL = 1024
PLE_DIM = 256
HEAD_DIM = 64
ROPE_DIM = HEAD_DIM // 4
ROPE_THETA = 500000.0
NORM_EPS = 1e-6
NEG_INF = -1e30

MOBA_HEADS = 8
MOBA_BLOCK = 256
MOBA_TOPK = 3
MOBA_WIDTH = MOBA_HEADS * HEAD_DIM

DIL_RATES = (1, 4, 16)
DIL_GROUPS = 3
DIL_HEADS = 8
DIL_WINDOW = 128
DIL_WIDTH = DIL_HEADS * HEAD_DIM
DIL_QBLOCK = 128
DIL_UNROLL = 4

SSM_INNER = D_MODEL
SSM_HEAD_DIM = 64
SSM_HEADS = SSM_INNER // SSM_HEAD_DIM
SSM_GROUPS = 4
SSM_STATE = 128
SSM_CONV = 4
SSM_CHUNK = 128
SSM_XBC = SSM_INNER + 2 * SSM_GROUPS * SSM_STATE

FFN_DIM = 2816
FFN_CONV = 3
FFN_CHUNK = 256
N_BRANCH = 3

COL_MOBA = 0
COL_DIL = 3 * MOBA_WIDTH
COL_Z = COL_DIL + 3 * DIL_GROUPS * DIL_WIDTH
COL_XBC = COL_Z + SSM_INNER
COL_DT = COL_XBC + SSM_XBC
COL_GATE = COL_DT + SSM_HEADS
IN_COLS = COL_GATE + N_BRANCH * D_MODEL
IN_TILE = 1536
QK_ROWS = 256
ATT_TILES = COL_Z // IN_TILE
MAIN_TILES = COL_DT // IN_TILE
GATE_TILES = N_BRANCH * D_MODEL // IN_TILE
MIX_Z = 0
MIX_XBC = SSM_INNER
MIX_GATE = SSM_INNER + SSM_XBC
MIX_COLS = MIX_GATE + N_BRANCH * D_MODEL

QSCALE = HEAD_DIM ** -0.5 * math.log2(math.e)

LANES = 128
HALO = 16
VMEM_LIMIT = 56 * 1024 * 1024


def _params(*sem):
    return pltpu.CompilerParams(dimension_semantics=sem, vmem_limit_bytes=VMEM_LIMIT)


def _sigmoid(x):
    return 1.0 / (1.0 + jnp.exp2(x * -math.log2(math.e)))


def _silu(x):
    return x * _sigmoid(x)


def _split3(a):
    a1 = a.astype(BF16)
    r1 = a - a1.astype(F32)
    a2 = r1.astype(BF16)
    a3 = (r1 - a2.astype(F32)).astype(BF16)
    return a1, a2, a3


def _dot_exact_rhs(a, b_exact, passes=3):
    out = None
    for piece in _split3(a)[:passes]:
        t = jnp.dot(piece, b_exact, preferred_element_type=F32)
        out = t if out is None else out + t
    return out


def _dot_exact_lhs(a_exact, b, passes=3):
    out = None
    for piece in _split3(b)[:passes]:
        t = jnp.dot(a_exact, piece, preferred_element_type=F32)
        out = t if out is None else out + t
    return out


def _head_norm_rope(x, bd, gcos, gup, gdn):
    ms = jnp.dot((x * x).astype(BF16), bd, preferred_element_type=F32)
    half = ROPE_DIM // 2
    rot = x * gcos + pltpu.roll(x, half, 1) * gup + pltpu.roll(x, LANES - half, 1) * gdn
    return rot * lax.rsqrt(ms + NORM_EPS)


def _rope_kernel(pos_ref, inv_ref, cos_ref, up_ref, dn_ref):
    ang = pos_ref[0] * inv_ref[...]
    d = lax.broadcasted_iota(jnp.int32, ang.shape, 1) % HEAD_DIM
    half = ROPE_DIM // 2
    s = jnp.sin(ang)
    cos_ref[0] = jnp.cos(ang)
    up_ref[0] = jnp.where((d >= half) & (d < ROPE_DIM), s, 0.0)
    dn_ref[0] = jnp.where(d < half, -s, 0.0)


def rope_tables(positions):
    bsz, seq = positions.shape
    ts = min(seq, 1024)
    d = np.arange(LANES) % HEAD_DIM
    inv = ROPE_THETA ** (-jnp.arange(0, ROPE_DIM, 2, dtype=F32) / ROPE_DIM)
    inv_lane = jnp.where(d < ROPE_DIM, inv[d % (ROPE_DIM // 2)], 0.0).astype(F32)[None, :]
    pos = positions.astype(F32)[..., None]
    shp = jax.ShapeDtypeStruct((bsz, seq, LANES), F32)
    spec = pl.BlockSpec((1, ts, LANES), lambda b, t: (b, t, 0))
    return pl.pallas_call(
        _rope_kernel, out_shape=(shp, shp, shp), grid=(bsz, seq // ts),
        in_specs=[pl.BlockSpec((1, ts, 1), lambda b, t: (b, t, 0)),
                  pl.BlockSpec((1, LANES), lambda b, t: (0, 0))],
        out_specs=(spec, spec, spec), compiler_params=_params("parallel", "parallel"),
        name="rope_tables")(pos, inv_lane)


def _inproj_kernel(x_ref, g_ref, w_ref, wg_ref, wdt_ref, hg_ref, bd_ref, cos_ref, up_ref, dn_ref,
                   att_ref, mix_ref, dt_ref, u_ref, *, tm):
    j = pl.program_id(1)

    @pl.when(j == 0)
    def _():
        u = _rms_bf16(x_ref[...], g_ref[...])
        u_ref[...] = u
        dt_ref[...] = jnp.dot(u, wdt_ref[...], preferred_element_type=F32)

    def attention_tile(gain_sets):
        att_ref[...] = jnp.dot(u_ref[...], w_ref[...], preferred_element_type=F32)
        bd = bd_ref[...]
        for r0 in range(0, tm, QK_ROWS):
            rows = slice(r0, r0 + QK_ROWS)
            for first, count in gain_sets:
                g0 = slice(first * LANES, (first + 1) * LANES)
                gcos = hg_ref[0:1, g0] * cos_ref[rows, :]
                gup = hg_ref[1:2, g0] * up_ref[rows, :]
                gdn = hg_ref[2:3, g0] * dn_ref[rows, :]
                for g in range(first, first + count):
                    lanes = slice(g * LANES, (g + 1) * LANES)
                    att_ref[rows, lanes] = _head_norm_rope(att_ref[rows, lanes], bd, gcos, gup, gdn)

    per_part = MOBA_WIDTH // LANES

    @pl.when(j == 0)
    def _():
        attention_tile([(0, per_part), (per_part, per_part)])

    @pl.when((j == 1) | (j == 2))
    def _():
        attention_tile([(0, IN_TILE // LANES)])

    @pl.when(j == 3)
    def _():
        attention_tile([])

    @pl.when((j >= ATT_TILES) & (j < MAIN_TILES))
    def _():
        mix_ref[...] = jnp.dot(u_ref[...], w_ref[...], preferred_element_type=F32).astype(BF16)

    @pl.when(j >= MAIN_TILES)
    def _():
        mix_ref[...] = jnp.dot(u_ref[...], wg_ref[...], preferred_element_type=F32).astype(BF16)


def in_projection(x2, gain, w_all, w_gate, w_dt, head_gain, bd, tables, tm=1024):
    n, d = x2.shape
    tm = min(tm, n)
    tn = IN_TILE
    assert 3 * MOBA_WIDTH == tn and DIL_GROUPS * DIL_WIDTH == tn and tm % QK_ROWS == 0
    tab = pl.BlockSpec((tm, LANES), lambda i, j: (i, 0))
    return pl.pallas_call(
        functools.partial(_inproj_kernel, tm=tm),
        out_shape=(jax.ShapeDtypeStruct((n, COL_Z), F32), jax.ShapeDtypeStruct((n, MIX_COLS), BF16),
                   jax.ShapeDtypeStruct((n, LANES), F32)),
        grid=(n // tm, MAIN_TILES + GATE_TILES),
        in_specs=[pl.BlockSpec((tm, d), lambda i, j: (i, 0)),
                  pl.BlockSpec((1, d), lambda i, j: (0, 0)),
                  pl.BlockSpec((d, tn), lambda i, j: (0, jnp.minimum(j, MAIN_TILES - 1))),
                  pl.BlockSpec((d, tn), lambda i, j: (0, jnp.maximum(j - MAIN_TILES, 0))),
                  pl.BlockSpec((d, LANES), lambda i, j: (0, 0)),
                  pl.BlockSpec((3, tn), lambda i, j: (0, jnp.minimum(j, ATT_TILES - 1))),
                  pl.BlockSpec((LANES, LANES), lambda i, j: (0, 0)), tab, tab, tab],
        out_specs=(pl.BlockSpec((tm, tn), lambda i, j: (i, jnp.minimum(j, ATT_TILES - 1))),
                   pl.BlockSpec((tm, tn), lambda i, j: (i, jnp.maximum(j - ATT_TILES, 0))),
                   pl.BlockSpec((tm, LANES), lambda i, j: (i, 0))),
        scratch_shapes=[pltpu.VMEM((tm, d), BF16)],
        compiler_params=_params("parallel", "arbitrary"), name="in_projection",
    )(x2, gain, w_all, w_gate, w_dt, head_gain, bd, *tables)


def _head_lanes(shape):
    lane = lax.broadcasted_iota(jnp.int32, shape, len(shape) - 1)
    return lane, (lane < HEAD_DIM, lane >= HEAD_DIM), (HEAD_DIM, 0)


def _moba_kernel(q_ref, k_ref, v_ref, o_ref, qa_ref, ka_ref, va_ref, km_ref, *, nb):
    blk = MOBA_BLOCK
    lane, head, aux = _head_lanes((blk, LANES))
    nbp = km_ref.shape[0]
    km_ref[...] = jnp.zeros(km_ref.shape, F32)

    def prep(j):
        rows = pl.ds(pl.multiple_of(j * blk, blk), blk)
        kn = k_ref[0, rows, :]
        v = v_ref[0, rows, :]
        km_ref[pl.ds(j, 1), :] = jnp.mean(kn, axis=0, keepdims=True)
        for h in range(2):
            ka_ref[h, rows, :] = jnp.where(head[h], kn, jnp.where(lane == aux[h] + j, 1.0, 0.0)).astype(BF16)
            va_ref[h, rows, :] = jnp.where(head[h], v, jnp.where(lane == aux[h], 1.0, 0.0)).astype(BF16)

    def prep2(t, carry):
        prep(2 * t)
        prep(2 * t + 1)
        return carry

    lax.fori_loop(0, nb // 2, prep2, 0)

    bidx = lax.broadcasted_iota(jnp.int32, (nbp, blk), 0).astype(F32)
    isblk = bidx < float(nb)
    _, head_k, _ = _head_lanes((nbp, LANES))
    nt = (((1,), (1,)), ((), ()))

    def select(i):
        rows = pl.ds(pl.multiple_of(i * blk, blk), blk)
        qf = q_ref[0, rows, :]
        q_hi = qf.astype(BF16)
        q_lo = (qf - q_hi.astype(F32)).astype(BF16)
        i_f = lax.convert_element_type(i, F32)
        for h in range(2):
            km = jnp.where(head_k[h], km_ref[...], 0.0)
            k_hi = km.astype(BF16)
            k_lo = (km - k_hi.astype(F32)).astype(BF16)
            dot = lambda a, b: lax.dot_general(a, b, nt, preferred_element_type=F32)
            sc = dot(k_hi, q_hi) + (dot(k_hi, q_lo) + dot(k_lo, q_hi))
            valid = isblk & (bidx < i_f)
            cur = jnp.where(valid, sc, -jnp.inf)
            sel = jnp.zeros((nbp, blk), F32)
            for _ in range(min(MOBA_TOPK, nb)):
                mx = jnp.max(cur, axis=0, keepdims=True)
                first = jnp.min(jnp.where((cur == mx) & isblk, bidx, float(nbp)), axis=0, keepdims=True)
                hit = bidx == first
                sel = jnp.where(hit, 1.0, sel)
                cur = jnp.where(hit, -jnp.inf, cur)
            keep = ((sel > 0.5) & valid) | (bidx == i_f)
            bias = jnp.where(isblk & jnp.logical_not(keep), NEG_INF, 0.0)
            pieces = [jnp.zeros((aux[h], blk), F32)] if aux[h] else []
            pieces += [bias, jnp.zeros((LANES - aux[h] - nbp, blk), F32)]
            bias_t = jnp.concatenate(pieces, axis=0).T
            qa_ref[h, rows, :] = jnp.where(head[h], qf * QSCALE, bias_t).astype(BF16)

    per_step = 4 if nb % 4 == 0 else 2

    def select_step(t, carry):
        for u in range(per_step):
            select(per_step * t + u)
        return carry

    lax.fori_loop(0, nb // per_step, select_step, 0)

    wide = 2 * blk
    _, head_w, _ = _head_lanes((wide, LANES))
    causal = (lax.broadcasted_iota(jnp.int32, (wide, wide), 1)
              <= lax.broadcasted_iota(jnp.int32, (wide, wide), 0))
    nt = (((1,), (1,)), ((), ()))

    def rows_of(t):
        return slice(t * wide, (t + 1) * wide)

    def logits(a, g):
        return [lax.dot_general(qa_ref[h, rows_of(a), :], ka_ref[h, rows_of(g), :], nt,
                                preferred_element_type=F32) for h in range(2)]

    tiles = [(a, g) for a in range(nb // 2) for g in [a] + list(range(a))]
    ss = logits(*tiles[0])
    state = None
    for t, (a, g) in enumerate(tiles):
        nxt = logits(*tiles[t + 1]) if t + 1 < len(tiles) else None
        new = []
        for h in range(2):
            s = ss[h]
            if g == a:
                s = jnp.where(causal, s, NEG_INF)
                m = jnp.max(s, axis=-1, keepdims=True)
                acc = jnp.dot(jnp.exp2(s - m).astype(BF16), va_ref[h, rows_of(g), :],
                              preferred_element_type=F32)
            else:
                m_old, acc_old = state[h]
                m = jnp.maximum(m_old, jnp.max(s, axis=-1, keepdims=True))
                acc = jnp.exp2(m_old - m) * acc_old + jnp.dot(
                    jnp.exp2(s - m).astype(BF16), va_ref[h, rows_of(g), :], preferred_element_type=F32)
            new.append((m, acc))
        state, ss = new, nxt
        if g == a - 1 or a == 0:
            den0 = state[0][1][:, aux[0]:aux[0] + 1]
            den1 = state[1][1][:, aux[1]:aux[1] + 1]
            o_ref[0, rows_of(a), :] = jnp.where(head_w[0], state[0][1] / den0, state[1][1] / den1)


def moba_attention(att3):
    bsz, seq, _ = att3.shape
    nb = seq // MOBA_BLOCK
    assert nb % 2 == 0 and nb <= HEAD_DIM, "key blocks are visited in pairs and indexed on 64 spare lanes"
    hp = MOBA_WIDTH // LANES
    qkv = lambda part: pl.BlockSpec((1, seq, LANES), lambda b, p, part=part: (b, 0, part * hp + p))
    return pl.pallas_call(
        functools.partial(_moba_kernel, nb=nb),
        out_shape=jax.ShapeDtypeStruct((bsz, seq, MOBA_WIDTH), F32), grid=(bsz, hp),
        in_specs=[qkv(0), qkv(1), qkv(2)],
        out_specs=pl.BlockSpec((1, seq, LANES), lambda b, p: (b, 0, p)),
        scratch_shapes=[pltpu.VMEM((2, seq, LANES), BF16), pltpu.VMEM((2, seq, LANES), BF16),
                        pltpu.VMEM((2, seq, LANES), BF16), pltpu.VMEM((-(-nb // 8) * 8, LANES), F32)],
        compiler_params=_params("parallel", "parallel"), name="moba_attention",
    )(att3, att3, att3)


def _dilated_kernel(q_ref, k_ref, v_ref, o_ref, lse_ref, qd_ref, kd_ref, va_ref, *, seq, rate):
    qb = DIL_QBLOCK
    cpb = seq // rate // qb
    shift = cpb.bit_length() - 1
    lane, head, aux = _head_lanes((qb, LANES))

    def token_rows(n):
        if rate == 1:
            return pl.ds(pl.multiple_of(n * qb, qb), qb)
        c, ch = lax.shift_right_logical(n, shift), n & (cpb - 1)
        return pl.ds(c + ch * (qb * rate), qb, stride=rate)

    def prep(n):
        src = token_rows(n)
        dst = pl.ds(pl.multiple_of(n * qb, qb), qb)
        kn = k_ref[0, src, :]
        v = v_ref[0, src, :]
        qd_ref[dst, :] = (q_ref[0, src, :] * QSCALE).astype(BF16)
        for h in range(2):
            kd_ref[h, dst, :] = jnp.where(head[h], kn, 0.0).astype(BF16)
            va_ref[h, dst, :] = jnp.concatenate(
                [jnp.where(head[h], v, 0.0), jnp.where(head[h], 1.0, 0.0)], axis=1).astype(BF16)

    def unrolled(fn):
        def step(it, carry):
            for u in range(DIL_UNROLL):
                fn(it * DIL_UNROLL + u)
            return carry
        lax.fori_loop(0, seq // qb // DIL_UNROLL, step, 0)

    unrolled(prep)

    rel = (lax.broadcasted_iota(jnp.int32, (qb, 2 * qb), 0)
           - lax.broadcasted_iota(jnp.int32, (qb, 2 * qb), 1))
    nt = (((1,), (1,)), ((), ()))

    def qblock(n):
        first = (n & (cpb - 1)) == 0
        k0 = jnp.where(first, n, n - 1) * qb
        krows = pl.ds(pl.multiple_of(k0, qb), 2 * qb)
        qrows = pl.ds(pl.multiple_of(n * qb, qb), qb)
        dist = rel + (n * qb - k0)
        ok = (dist >= 0) & (dist <= DIL_WINDOW)
        keys = jnp.concatenate([kd_ref[0, krows, :], kd_ref[1, krows, :]], axis=0)
        s = lax.dot_general(qd_ref[qrows, :], keys, nt, preferred_element_type=F32)
        ms, ps = [], []
        for h in range(2):
            sh = jnp.where(ok, s[:, h * 2 * qb:(h + 1) * 2 * qb], NEG_INF)
            ms.append(jnp.max(sh, axis=-1, keepdims=True))
            ps.append(jnp.exp2(sh - ms[h]).astype(BF16))
        vals = jnp.concatenate([va_ref[0, krows, :], va_ref[1, krows, :]], axis=0)
        acc = jnp.dot(jnp.concatenate(ps, axis=1), vals, preferred_element_type=F32)
        den = acc[:, LANES:]
        dst = token_rows(n)
        o_ref[0, dst, :] = acc[:, :LANES] / den
        lse_ref[0, dst, :] = jnp.where(head[0], ms[0], ms[1]) + jnp.log2(den)

    unrolled(qblock)


def dilated_group(att3, group):
    bsz, seq, cols = att3.shape
    rate = DIL_RATES[group]
    cpb = seq // rate // DIL_QBLOCK
    assert cpb >= 2 and cpb & (cpb - 1) == 0 and (seq // DIL_QBLOCK) % DIL_UNROLL == 0
    hp = DIL_WIDTH // LANES
    base = COL_DIL // LANES
    qkv = lambda part: pl.BlockSpec(
        (1, seq, LANES), lambda b, p, part=part: (b, 0, base + (part * DIL_GROUPS + group) * hp + p))
    out = pl.BlockSpec((1, seq, LANES), lambda b, p: (b, 0, p))
    shp = jax.ShapeDtypeStruct((bsz, seq, DIL_WIDTH), F32)
    return pl.pallas_call(
        functools.partial(_dilated_kernel, seq=seq, rate=rate), out_shape=(shp, shp), grid=(bsz, hp),
        in_specs=[qkv(0), qkv(1), qkv(2)],
        out_specs=(out, out),
        scratch_shapes=[pltpu.VMEM((seq, LANES), BF16), pltpu.VMEM((2, seq, LANES), BF16),
                        pltpu.VMEM((2, seq, 2 * LANES), BF16)],
        compiler_params=_params("parallel", "parallel"), name=f"dilated_rate{rate}",
    )(att3, att3, att3)


def _ssd_kernel(z_ref, xs_ref, bc_ref, dt_ref, cwx_ref, cwb_ref, cbx_ref, cbb_ref, dtb_ref, alog_ref,
                dexp_ref, onorm_ref, tri_ref, triu_ref, exp_ref, o_ref, xpx_ref, xpb_ref, st_ref):
    L = SSM_CHUNK
    pad = 8

    @pl.when(pl.program_id(1) == 0)
    def _():
        xpx_ref[0:pad, :] = jnp.zeros((pad, SSM_INNER), F32)
        xpb_ref[0:pad, :] = jnp.zeros((pad, SSM_INNER), F32)
        st_ref[...] = jnp.zeros(st_ref.shape, F32)

    def conv_silu(src_ref, pad_ref, w_ref, b_ref):
        pad_ref[pad:, :] = src_ref[0].astype(F32)
        acc = b_ref[...] + w_ref[0:1, :] * pad_ref[pl.ds(pad - SSM_CONV + 1, L), :]
        for k in range(1, SSM_CONV):
            acc = acc + w_ref[k:k + 1, :] * pad_ref[pl.ds(pad - SSM_CONV + 1 + k, L), :]
        pad_ref[0:pad, :] = pad_ref[L:L + pad, :]
        return _silu(acc)

    xs = conv_silu(xs_ref, xpx_ref, cwx_ref, cbx_ref)
    bc = conv_silu(bc_ref, xpb_ref, cwb_ref, cbb_ref)
    gn = SSM_GROUPS * SSM_STATE
    bm, cm = bc[:, :gn], bc[:, gn:]

    xr = dt_ref[0] + dtb_ref[...]
    dt = jnp.maximum(xr, 0.0) + jnp.log(1.0 + jnp.exp(-jnp.abs(xr)))
    adt = dt * (-jnp.exp(alog_ref[...]) * math.log2(math.e))
    acs = _dot_exact_lhs(tri_ref[...], adt)
    acs_t = _dot_exact_rhs(adt.T, triu_ref[...])
    expand = exp_ref[...]
    dt_e = _dot_exact_rhs(dt, expand)
    acs_e = _dot_exact_rhs(acs, expand)
    xdt = xs * dt_e
    last = acs_e[L - 1:L, :]
    grow = jnp.exp2(acs_e)
    to_end = jnp.exp2(last - acs_e)
    chunk_decay = jnp.exp2(last)

    ll = lax.broadcasted_iota(jnp.int32, (L, L), 0)
    ss = lax.broadcasted_iota(jnp.int32, (L, L), 1)
    causal = ll >= ss
    gw = SSM_INNER // SSM_GROUPS
    hpg = SSM_HEADS // SSM_GROUPS
    lane = lax.broadcasted_iota(jnp.int32, (L, gw), 1)
    nt = (((1,), (1,)), ((), ()))
    tn = (((0,), (0,)), ((), ()))
    ys = []
    for g in range(SSM_GROUPS):
        cols = slice(g * gw, (g + 1) * gw)
        bg = bm[:, g * SSM_STATE:(g + 1) * SSM_STATE].astype(BF16)
        cg = cm[:, g * SSM_STATE:(g + 1) * SSM_STATE].astype(BF16)
        xg = xdt[:, cols]
        cb = lax.dot_general(cg, bg, nt, preferred_element_type=F32)
        st = st_ref[g]
        y = jnp.dot(cg, st.astype(BF16), preferred_element_type=F32) * grow[:, cols]
        new = lax.dot_general(bg, (xg * to_end[:, cols]).astype(BF16), tn, preferred_element_type=F32)
        st_ref[g] = chunk_decay[:, cols] * st + new
        for hh in range(hpg):
            h = g * hpg + hh
            diff = acs[:, h:h + 1] - acs_t[h:h + 1, :]
            mat = (cb * jnp.exp2(jnp.where(causal, diff, -jnp.inf))).astype(BF16)
            xh = jnp.where((lane >= hh * SSM_HEAD_DIM) & (lane < (hh + 1) * SSM_HEAD_DIM), xg, 0.0)
            y = y + jnp.dot(mat, xh.astype(BF16), preferred_element_type=F32)
        ys.append(y)
    y = jnp.concatenate(ys, axis=1) + xs * dexp_ref[...]
    yg = y * _silu(z_ref[0].astype(F32))
    ms = jnp.mean(yg * yg, axis=-1, keepdims=True)
    o_ref[0] = yg * lax.rsqrt(ms + NORM_EPS) * onorm_ref[...]


def ssd_mixer(mix3, dt3, conv_w, conv_b, dt_bias, a_log, d_exp, out_norm, consts):
    bsz, seq, _ = mix3.shape
    L = SSM_CHUNK
    w = SSM_INNER
    tri, triu, expand = consts
    col = lambda idx: pl.BlockSpec((1, L, w), lambda b, c, idx=idx: (b, c, idx))
    vecw = lambda rows, idx: pl.BlockSpec((rows, w), lambda b, c, idx=idx: (0, idx))
    vec = pl.BlockSpec((1, LANES), lambda b, c: (0, 0))
    sq = pl.BlockSpec((L, L), lambda b, c: (0, 0))
    return pl.pallas_call(
        _ssd_kernel, out_shape=jax.ShapeDtypeStruct((bsz, seq, w), F32), grid=(bsz, seq // L),
        in_specs=[col(MIX_Z // w), col(MIX_XBC // w), col(MIX_XBC // w + 1),
                  pl.BlockSpec((1, L, LANES), lambda b, c: (b, c, 0)),
                  vecw(SSM_CONV, 0), vecw(SSM_CONV, 1), vecw(1, 0), vecw(1, 1), vec, vec,
                  vecw(1, 0), vecw(1, 0), sq, sq, pl.BlockSpec((LANES, w), lambda b, c: (0, 0))],
        out_specs=pl.BlockSpec((1, L, w), lambda b, c: (b, c, 0)),
        scratch_shapes=[pltpu.VMEM((L + 8, w), F32), pltpu.VMEM((L + 8, w), F32),
                        pltpu.VMEM((SSM_GROUPS, SSM_STATE, w // SSM_GROUPS), F32)],
        compiler_params=_params("parallel", "arbitrary"), name="ssd_mixer",
    )(mix3, mix3, mix3, dt3, conv_w, conv_w, conv_b, conv_b, dt_bias, a_log, d_exp, out_norm,
      tri, triu, expand)


def _merge_kernel(x_ref, a_ref, m_ref, o0_ref, o1_ref, o2_ref, l0_ref, l1_ref, l2_ref, gl_ref, bg_ref,
                  wa_ref, wm_ref, wc_ref, wo_ref, out_ref):
    l0, l1, l2 = l0_ref[...], l1_ref[...], l2_ref[...]
    lmax = jnp.maximum(jnp.maximum(l0, l1), l2)
    e0, e1, e2 = jnp.exp2(l0 - lmax), jnp.exp2(l1 - lmax), jnp.exp2(l2 - lmax)
    cmix = (e0 * o0_ref[...] + e1 * o1_ref[...] + e2 * o2_ref[...]) / (e0 + e1 + e2)
    gates = _sigmoid(gl_ref[...] + bg_ref[...])
    d = D_MODEL
    mm = lambda v, w_ref: jnp.dot(v.astype(BF16), w_ref[...], preferred_element_type=F32)
    merged = (gates[:, :d] * mm(a_ref[...], wa_ref) + gates[:, d:2 * d] * mm(m_ref[...], wm_ref)
              + gates[:, 2 * d:] * mm(cmix, wc_ref))
    out_ref[...] = x_ref[...] + mm(merged, wo_ref)


def merge_branches(x2, a2, m2, dil, mix2, b_gate, wa, wm, wc, wo, tm=512):
    n, d = x2.shape
    tm = min(tm, n)
    row = lambda width, idx=0: pl.BlockSpec((tm, width), lambda i, idx=idx: (i, idx))
    full = lambda arr: pl.BlockSpec(arr.shape, lambda i: (0, 0), pipeline_mode=pl.Buffered(1))
    (o0, l0), (o1, l1), (o2, l2) = dil
    gw = N_BRANCH * d
    return pl.pallas_call(
        _merge_kernel, out_shape=jax.ShapeDtypeStruct((n, d), F32), grid=(n // tm,),
        in_specs=[row(d), row(MOBA_WIDTH), row(SSM_INNER)] + [row(DIL_WIDTH)] * 6
                 + [row(gw, MIX_GATE // gw), full(b_gate), full(wa), full(wm), full(wc), full(wo)],
        out_specs=row(d), compiler_params=_params("parallel"), name="merge_branches",
    )(x2, a2, m2, o0, o1, o2, l0, l1, l2, mix2, b_gate, wa, wm, wc, wo)


def _rms_bf16(x, gain):
    ms = jnp.mean(x * x, axis=-1, keepdims=True)
    return (x * lax.rsqrt(ms + NORM_EPS) * gain).astype(BF16)


def _ffn_ple_kernel(x_ref, xh_ref, p_ref, gf_ref, wup_ref, cw_ref, cb_ref, wd_ref, gp_ref, wg_ref, wp_ref,
                    o_ref, u_ref, buf_ref, h_ref, *, tm, tiles_per_seq):
    ck = FFN_CHUNK
    nck = FFN_DIM // ck
    u_ref[0:HALO, :] = _rms_bf16(xh_ref[...], gf_ref[...])
    u_ref[HALO:, :] = _rms_bf16(x_ref[...], gf_ref[...])
    keep = jnp.where(pl.program_id(0) % tiles_per_seq == 0, 0.0, 1.0)

    def up(c):
        u = u_ref[...]
        for half in range(2):
            buf = buf_ref.at[2 * (c % 2) + half]
            cols = slice(half * FFN_DIM + c * ck, half * FFN_DIM + (c + 1) * ck)
            val = jnp.dot(u, wup_ref[:, cols], preferred_element_type=F32)
            buf[0:HALO, :] = val[0:HALO, :] * keep
            buf[HALO:, :] = val[HALO:, :]

    def conv(c, half):
        buf = buf_ref.at[2 * (c % 2) + half]
        cols = slice(half * FFN_DIM + c * ck, half * FFN_DIM + (c + 1) * ck)
        acc = cb_ref[:, cols] + cw_ref[0:1, cols] * buf[pl.ds(HALO - FFN_CONV + 1, tm), :]
        for k in range(1, FFN_CONV):
            acc = acc + cw_ref[k:k + 1, cols] * buf[pl.ds(HALO - FFN_CONV + 1 + k, tm), :]
        return acc

    up(0)
    for c in range(nck):
        if c + 1 < nck:
            up(c + 1)
        h_ref[:, c * ck:(c + 1) * ck] = (_silu(conv(c, 0)) * conv(c, 1)).astype(BF16)
    acc = x_ref[...] + jnp.dot(h_ref[...], wd_ref[...], preferred_element_type=F32)
    pg = _sigmoid(jnp.dot(_rms_bf16(acc, gp_ref[...]), wg_ref[...], preferred_element_type=F32))
    o_ref[...] = acc + jnp.dot(p_ref[...].astype(BF16), wp_ref[...], preferred_element_type=F32) * pg


def ffn_ple(x2, p_all, layer, gain_ffn, w_up, conv_w, conv_b, w_down, gain_ple, w_gate, w_ple, seq, tm=512):
    n, d = x2.shape
    tm = min(tm, seq)
    hb = tm // HALO
    row = lambda width: pl.BlockSpec((tm, width), lambda i: (i, 0))
    p_rows = pl.BlockSpec((tm, PLE_DIM), lambda i: (layer * (n // tm) + i, 0))
    resident = lambda arr: pl.BlockSpec(arr.shape, lambda i: (0, 0), pipeline_mode=pl.Buffered(1))
    return pl.pallas_call(
        functools.partial(_ffn_ple_kernel, tm=tm, tiles_per_seq=seq // tm),
        out_shape=jax.ShapeDtypeStruct((n, d), F32), grid=(n // tm,),
        in_specs=[row(d), pl.BlockSpec((HALO, d), lambda i: (jnp.maximum(i * hb - 1, 0), 0)), p_rows,
                  resident(gain_ffn), resident(w_up), resident(conv_w), resident(conv_b), resident(w_down),
                  resident(gain_ple), resident(w_gate), resident(w_ple)],
        out_specs=row(d),
        scratch_shapes=[pltpu.VMEM((tm + HALO, d), BF16), pltpu.VMEM((4, tm + HALO, FFN_CHUNK), F32),
                        pltpu.VMEM((tm, FFN_DIM), BF16)],
        compiler_params=_params("parallel"), name="ffn_ple",
    )(x2, x2, p_all, gain_ffn, w_up, conv_w, conv_b, w_down, gain_ple, w_gate, w_ple)


def _constants():
    lane = np.arange(LANES)
    bd = (lane[:, None] // HEAD_DIM == lane[None, :] // HEAD_DIM).astype(np.float32) / HEAD_DIM
    r = np.arange(SSM_CHUNK)
    tri = (r[None, :] <= r[:, None]).astype(np.float32)
    expand = (lane[:, None] == (np.arange(SSM_INNER)[None, :] // SSM_HEAD_DIM)).astype(np.float32)
    as_bf16 = lambda a: jnp.asarray(a, dtype=BF16)
    return as_bf16(bd), (as_bf16(tri), as_bf16(tri.T), as_bf16(expand))


def _pad_lanes(v):
    return jnp.pad(v, (0, LANES - v.shape[0]))[None, :]


def kernel(x, p, positions, norm_mix, w_in, b_gate, moba_q_norm, moba_k_norm, dil_q_norm, dil_k_norm,
           ssm_conv_w, ssm_conv_b, ssm_dt_bias, ssm_a_log, ssm_d, ssm_out_norm, w_br_moba, w_br_ssm,
           w_br_dil, w_out, norm_ffn, w_up, ffn_conv_w, ffn_conv_b, w_down, norm_ple, w_ple_gate, w_ple):
    bsz, seq, d = x.shape
    depth = w_in.shape[0]
    n = bsz * seq
    bd, ssd_consts = _constants()
    tables = [t.reshape(n, LANES) for t in rope_tables(positions)]
    row = lambda v: v[None, :]
    heads = lambda v, count: jnp.tile(v, count)
    ones = lambda count: jnp.ones((count,), F32)

    x2 = x.reshape(n, d)
    for i in range(depth):
        w_all = w_in[i].astype(BF16)
        w_dt = jnp.pad(w_all[:, COL_DT:COL_GATE], ((0, 0), (0, LANES - SSM_HEADS)))
        half = ROPE_DIM // 2
        head_gain = jnp.stack([jnp.concatenate([
            heads(jnp.roll(moba_q_norm[i], s), MOBA_HEADS), heads(jnp.roll(moba_k_norm[i], s), MOBA_HEADS),
            ones(MOBA_WIDTH), heads(jnp.roll(dil_q_norm[i], s), DIL_GROUPS * DIL_HEADS),
            heads(jnp.roll(dil_k_norm[i], s), DIL_GROUPS * DIL_HEADS), ones(DIL_GROUPS * DIL_WIDTH)])
            for s in (0, half, -half)])
        att2, mix2, dt2 = in_projection(x2, row(norm_mix[i]), w_all, w_all[:, COL_GATE:], w_dt,
                                        head_gain, bd, tables)
        att3 = att2.reshape(bsz, seq, COL_Z)

        out_a = moba_attention(att3)
        out_b = ssd_mixer(mix2.reshape(bsz, seq, MIX_COLS), dt2.reshape(bsz, seq, LANES), ssm_conv_w[i],
                          row(ssm_conv_b[i]), _pad_lanes(ssm_dt_bias[i]), _pad_lanes(ssm_a_log[i]),
                          row(jnp.repeat(ssm_d[i], SSM_HEAD_DIM)), row(ssm_out_norm[i]), ssd_consts)
        dil = [dilated_group(att3, g) for g in range(DIL_GROUPS)]
        dil2 = [(o.reshape(n, DIL_WIDTH), l.reshape(n, DIL_WIDTH)) for o, l in dil]

        x2 = merge_branches(x2, out_a.reshape(n, MOBA_WIDTH), out_b.reshape(n, SSM_INNER), dil2, mix2,
                            row(b_gate[i]), w_br_moba[i].astype(BF16), w_br_ssm[i].astype(BF16),
                            w_br_dil[i].astype(BF16), w_out[i].astype(BF16))
        x2 = ffn_ple(x2, p.reshape(depth * n, PLE_DIM), i, row(norm_ffn[i]), w_up[i].astype(BF16), ffn_conv_w[i],
                     row(ffn_conv_b[i]), w_down[i].astype(BF16), row(norm_ple[i]),
                     w_ple_gate[i].astype(BF16), w_ple[i].astype(BF16), seq)
    return x2.reshape(bsz, seq, d)
```

```python
import functools
import math

import numpy as np
import jax
import jax.numpy as jnp
from jax import lax
from jax.experimental import pallas as pl
from jax.experimental.pallas import tpu as pltpu

F32 = jnp.float32
BF16 = jnp.bfloat16

D_MODEL = 1024
PLE_DIM = 256
HEAD_DIM = 64
ROPE_DIM = HEAD_DIM // 4
ROPE_THETA = 500000.0
NORM_EPS = 1e-6
NEG_INF = -1e30

MOBA_HEADS = 8
MOBA_BLOCK = 256
MOBA_TOPK = 3
MOBA_WIDTH = MOBA_HEADS * HEAD_DIM

DIL_RATES = (1, 4, 16)
DIL_GROUPS = 3
DIL_HEADS = 8
DIL_WINDOW = 128
DIL_WIDTH = DIL_HEADS * HEAD_DIM
DIL_QBLOCK = 128
DIL_UNROLL = 4

SSM_INNER = D_MODEL
SSM_HEAD_DIM = 64
SSM_HEADS = SSM_INNER // SSM_HEAD_DIM
SSM_GROUPS = 4
SSM_STATE = 128
SSM_CONV = 4
SSM_CHUNK = 128
SSM_XBC = SSM_INNER + 2 * SSM_GROUPS * SSM_STATE
SSD_STEP_CHUNKS = 2
SSD_PAD = 8

FFN_DIM = 2816
FFN_CONV = 3
FFN_CHUNK = 256
N_BRANCH = 3

COL_MOBA = 0
COL_DIL = 3 * MOBA_WIDTH
COL_Z = COL_DIL + 3 * DIL_GROUPS * DIL_WIDTH
COL_XBC = COL_Z + SSM_INNER
COL_DT = COL_XBC + SSM_XBC
COL_GATE = COL_DT + SSM_HEADS
IN_COLS = COL_GATE + N_BRANCH * D_MODEL
IN_TILE = 1536
QK_ROWS = 256
ATT_TILES = COL_Z // IN_TILE
MAIN_TILES = COL_DT // IN_TILE
GATE_TILES = N_BRANCH * D_MODEL // IN_TILE
MIX_Z = 0
MIX_XBC = SSM_INNER
MIX_GATE = SSM_INNER + SSM_XBC
MIX_COLS = MIX_GATE + N_BRANCH * D_MODEL

QSCALE = HEAD_DIM ** -0.5 * math.log2(math.e)

LANES = 128
HALO = 16
VMEM_LIMIT = 56 * 1024 * 1024


def _params(*sem):
    return pltpu.CompilerParams(dimension_semantics=sem, vmem_limit_bytes=VMEM_LIMIT)


def _sigmoid(x):
    return 1.0 / (1.0 + jnp.exp2(x * -math.log2(math.e)))


def _silu(x):
    return x * _sigmoid(x)


def _split3(a):
    a1 = a.astype(BF16)
    r1 = a - a1.astype(F32)
    a2 = r1.astype(BF16)
    a3 = (r1 - a2.astype(F32)).astype(BF16)
    return a1, a2, a3


def _dot_exact_rhs(a, b_exact, passes=3):
    out = None
    for piece in _split3(a)[:passes]:
        t = jnp.dot(piece, b_exact, preferred_element_type=F32)
        out = t if out is None else out + t
    return out


def _dot_exact_lhs(a_exact, b, passes=3):
    out = None
    for piece in _split3(b)[:passes]:
        t = jnp.dot(a_exact, piece, preferred_element_type=F32)
        out = t if out is None else out + t
    return out


def _head_norm_rope(x, bd, gcos, gup, gdn):
    ms = jnp.dot((x * x).astype(BF16), bd, preferred_element_type=F32)
    half = ROPE_DIM // 2
    rot = x * gcos + pltpu.roll(x, half, 1) * gup + pltpu.roll(x, LANES - half, 1) * gdn
    return rot * lax.rsqrt(ms + NORM_EPS)


def _rope_kernel(pos_ref, inv_ref, cos_ref, up_ref, dn_ref):
    ang = pos_ref[0] * inv_ref[...]
    d = lax.broadcasted_iota(jnp.int32, ang.shape, 1) % HEAD_DIM
    half = ROPE_DIM // 2
    s = jnp.sin(ang)
    cos_ref[0] = jnp.cos(ang)
    up_ref[0] = jnp.where((d >= half) & (d < ROPE_DIM), s, 0.0)
    dn_ref[0] = jnp.where(d < half, -s, 0.0)


def rope_tables(positions):
    bsz, seq = positions.shape
    ts = min(seq, 1024)
    d = np.arange(LANES) % HEAD_DIM
    inv = ROPE_THETA ** (-jnp.arange(0, ROPE_DIM, 2, dtype=F32) / ROPE_DIM)
    inv_lane = jnp.where(d < ROPE_DIM, inv[d % (ROPE_DIM // 2)], 0.0).astype(F32)[None, :]
    pos = positions.astype(F32)[..., None]
    shp = jax.ShapeDtypeStruct((bsz, seq, LANES), F32)
    spec = pl.BlockSpec((1, ts, LANES), lambda b, t: (b, t, 0))
    return pl.pallas_call(
        _rope_kernel, out_shape=(shp, shp, shp), grid=(bsz, seq // ts),
        in_specs=[pl.BlockSpec((1, ts, 1), lambda b, t: (b, t, 0)),
                  pl.BlockSpec((1, LANES), lambda b, t: (0, 0))],
        out_specs=(spec, spec, spec), compiler_params=_params("parallel", "parallel"),
        name="rope_tables")(pos, inv_lane)


def _inproj_kernel(x_ref, g_ref, w_ref, wg_ref, wdt_ref, hg_ref, bd_ref, cos_ref, up_ref, dn_ref,
                   att_ref, mix_ref, dt_ref, u_ref, *, tm):
    j = pl.program_id(1)

    @pl.when(j == 0)
    def _():
        u = _rms_bf16(x_ref[...], g_ref[...])
        u_ref[...] = u
        dt_ref[...] = jnp.dot(u, wdt_ref[...], preferred_element_type=F32)

    def attention_tile(gain_sets):
        att_ref[...] = jnp.dot(u_ref[...], w_ref[...], preferred_element_type=F32)
        bd = bd_ref[...]
        for r0 in range(0, tm, QK_ROWS):
            rows = slice(r0, r0 + QK_ROWS)
            for first, count in gain_sets:
                g0 = slice(first * LANES, (first + 1) * LANES)
                gcos = hg_ref[0:1, g0] * cos_ref[rows, :]
                gup = hg_ref[1:2, g0] * up_ref[rows, :]
                gdn = hg_ref[2:3, g0] * dn_ref[rows, :]
                for g in range(first, first + count):
                    lanes = slice(g * LANES, (g + 1) * LANES)
                    att_ref[rows, lanes] = _head_norm_rope(att_ref[rows, lanes], bd, gcos, gup, gdn)

    per_part = MOBA_WIDTH // LANES

    @pl.when(j == 0)
    def _():
        attention_tile([(0, per_part), (per_part, per_part)])

    @pl.when((j == 1) | (j == 2))
    def _():
        attention_tile([(0, IN_TILE // LANES)])

    @pl.when(j == 3)
    def _():
        attention_tile([])

    @pl.when((j >= ATT_TILES) & (j < MAIN_TILES))
    def _():
        mix_ref[...] = jnp.dot(u_ref[...], w_ref[...], preferred_element_type=F32).astype(BF16)

    @pl.when(j >= MAIN_TILES)
    def _():
        mix_ref[...] = jnp.dot(u_ref[...], wg_ref[...], preferred_element_type=F32).astype(BF16)


def in_projection(x2, gain, w_all, w_gate, w_dt, head_gain, bd, tables, tm=1024):
    n, d = x2.shape
    tm = min(tm, n)
    tn = IN_TILE
    assert 3 * MOBA_WIDTH == tn and DIL_GROUPS * DIL_WIDTH == tn and tm % QK_ROWS == 0
    tab = pl.BlockSpec((tm, LANES), lambda i, j: (i, 0))
    return pl.pallas_call(
        functools.partial(_inproj_kernel, tm=tm),
        out_shape=(jax.ShapeDtypeStruct((n, COL_Z), F32), jax.ShapeDtypeStruct((n, MIX_COLS), BF16),
                   jax.ShapeDtypeStruct((n, LANES), F32)),
        grid=(n // tm, MAIN_TILES + GATE_TILES),
        in_specs=[pl.BlockSpec((tm, d), lambda i, j: (i, 0)),
                  pl.BlockSpec((1, d), lambda i, j: (0, 0)),
                  pl.BlockSpec((d, tn), lambda i, j: (0, jnp.minimum(j, MAIN_TILES - 1))),
                  pl.BlockSpec((d, tn), lambda i, j: (0, jnp.maximum(j - MAIN_TILES, 0))),
                  pl.BlockSpec((d, LANES), lambda i, j: (0, 0)),
                  pl.BlockSpec((3, tn), lambda i, j: (0, jnp.minimum(j, ATT_TILES - 1))),
                  pl.BlockSpec((LANES, LANES), lambda i, j: (0, 0)), tab, tab, tab],
        out_specs=(pl.BlockSpec((tm, tn), lambda i, j: (i, jnp.minimum(j, ATT_TILES - 1))),
                   pl.BlockSpec((tm, tn), lambda i, j: (i, jnp.maximum(j - ATT_TILES, 0))),
                   pl.BlockSpec((tm, LANES), lambda i, j: (i, 0))),
        scratch_shapes=[pltpu.VMEM((tm, d), BF16)],
        compiler_params=_params("parallel", "arbitrary"), name="in_projection",
    )(x2, gain, w_all, w_gate, w_dt, head_gain, bd, *tables)


def _head_lanes(shape):
    lane = lax.broadcasted_iota(jnp.int32, shape, len(shape) - 1)
    return lane, (lane < HEAD_DIM, lane >= HEAD_DIM), (HEAD_DIM, 0)


def _moba_kernel(q_ref, k_ref, v_ref, o_ref, qa_ref, ka_ref, va_ref, km_ref, *, nb):
    blk = MOBA_BLOCK
    lane, head, aux = _head_lanes((blk, LANES))
    nbp = km_ref.shape[0]
    km_ref[...] = jnp.zeros(km_ref.shape, F32)

    def prep(j):
        rows = pl.ds(pl.multiple_of(j * blk, blk), blk)
        kn = k_ref[0, rows, :]
        v = v_ref[0, rows, :]
        km_ref[pl.ds(j, 1), :] = jnp.mean(kn, axis=0, keepdims=True)
        for h in range(2):
            ka_ref[h, rows, :] = jnp.where(head[h], kn, jnp.where(lane == aux[h] + j, 1.0, 0.0)).astype(BF16)
            va_ref[h, rows, :] = jnp.where(head[h], v, jnp.where(lane == aux[h], 1.0, 0.0)).astype(BF16)

    def prep2(t, carry):
        prep(2 * t)
        prep(2 * t + 1)
        return carry

    lax.fori_loop(0, nb // 2, prep2, 0)

    bidx = lax.broadcasted_iota(jnp.int32, (nbp, blk), 0).astype(F32)
    isblk = bidx < float(nb)
    _, head_k, _ = _head_lanes((nbp, LANES))
    nt = (((1,), (1,)), ((), ()))

    def select(i):
        rows = pl.ds(pl.multiple_of(i * blk, blk), blk)
        qf = q_ref[0, rows, :]
        q_hi = qf.astype(BF16)
        q_lo = (qf - q_hi.astype(F32)).astype(BF16)
        i_f = lax.convert_element_type(i, F32)
        for h in range(2):
            km = jnp.where(head_k[h], km_ref[...], 0.0)
            k_hi = km.astype(BF16)
            k_lo = (km - k_hi.astype(F32)).astype(BF16)
            dot = lambda a, b: lax.dot_general(a, b, nt, preferred_element_type=F32)
            sc = dot(k_hi, q_hi) + (dot(k_hi, q_lo) + dot(k_lo, q_hi))
            valid = isblk & (bidx < i_f)
            cur = jnp.where(valid, sc, -jnp.inf)
            sel = jnp.zeros((nbp, blk), F32)
            for _ in range(min(MOBA_TOPK, nb)):
                mx = jnp.max(cur, axis=0, keepdims=True)
                first = jnp.min(jnp.where((cur == mx) & isblk, bidx, float(nbp)), axis=0, keepdims=True)
                hit = bidx == first
                sel = jnp.where(hit, 1.0, sel)
                cur = jnp.where(hit, -jnp.inf, cur)
            keep = ((sel > 0.5) & valid) | (bidx == i_f)
            bias = jnp.where(isblk & jnp.logical_not(keep), NEG_INF, 0.0)
            pieces = [jnp.zeros((aux[h], blk), F32)] if aux[h] else []
            pieces += [bias, jnp.zeros((LANES - aux[h] - nbp, blk), F32)]
            bias_t = jnp.concatenate(pieces, axis=0).T
            qa_ref[h, rows, :] = jnp.where(head[h], qf * QSCALE, bias_t).astype(BF16)

    per_step = 4 if nb % 4 == 0 else 2

    def select_step(t, carry):
        for u in range(per_step):
            select(per_step * t + u)
        return carry

    lax.fori_loop(0, nb // per_step, select_step, 0)

    wide = 2 * blk
    _, head_w, _ = _head_lanes((wide, LANES))
    causal = (lax.broadcasted_iota(jnp.int32, (wide, wide), 1)
              <= lax.broadcasted_iota(jnp.int32, (wide, wide), 0))
    nt = (((1,), (1,)), ((), ()))

    def rows_of(t):
        return slice(t * wide, (t + 1) * wide)

    def logits(a, g):
        return [lax.dot_general(qa_ref[h, rows_of(a), :], ka_ref[h, rows_of(g), :], nt,
                                preferred_element_type=F32) for h in range(2)]

    tiles = [(a, g) for a in range(nb // 2) for g in [a] + list(range(a))]
    ss = logits(*tiles[0])
    state = None
    for t, (a, g) in enumerate(tiles):
        nxt = logits(*tiles[t + 1]) if t + 1 < len(tiles) else None
        new = []
        for h in range(2):
            s = ss[h]
            if g == a:
                s = jnp.where(causal, s, NEG_INF)
                m = jnp.max(s, axis=-1, keepdims=True)
                acc = jnp.dot(jnp.exp2(s - m).astype(BF16), va_ref[h, rows_of(g), :],
                              preferred_element_type=F32)
            else:
                m_old, acc_old = state[h]
                m = jnp.maximum(m_old, jnp.max(s, axis=-1, keepdims=True))
                acc = jnp.exp2(m_old - m) * acc_old + jnp.dot(
                    jnp.exp2(s - m).astype(BF16), va_ref[h, rows_of(g), :], preferred_element_type=F32)
            new.append((m, acc))
        state, ss = new, nxt
        if g == a - 1 or a == 0:
            den0 = state[0][1][:, aux[0]:aux[0] + 1]
            den1 = state[1][1][:, aux[1]:aux[1] + 1]
            o_ref[0, rows_of(a), :] = jnp.where(head_w[0], state[0][1] / den0, state[1][1] / den1)


def moba_attention(att3):
    bsz, seq, _ = att3.shape
    nb = seq // MOBA_BLOCK
    assert nb % 2 == 0 and nb <= HEAD_DIM, "key blocks are visited in pairs and indexed on 64 spare lanes"
    hp = MOBA_WIDTH // LANES
    qkv = lambda part: pl.BlockSpec((1, seq, LANES), lambda b, p, part=part: (b, 0, part * hp + p))
    return pl.pallas_call(
        functools.partial(_moba_kernel, nb=nb),
        out_shape=jax.ShapeDtypeStruct((bsz, seq, MOBA_WIDTH), F32), grid=(bsz, hp),
        in_specs=[qkv(0), qkv(1), qkv(2)],
        out_specs=pl.BlockSpec((1, seq, LANES), lambda b, p: (b, 0, p)),
        scratch_shapes=[pltpu.VMEM((2, seq, LANES), BF16), pltpu.VMEM((2, seq, LANES), BF16),
                        pltpu.VMEM((2, seq, LANES), BF16), pltpu.VMEM((-(-nb // 8) * 8, LANES), F32)],
        compiler_params=_params("parallel", "parallel"), name="moba_attention",
    )(att3, att3, att3)


def _dilated_kernel(q_ref, k_ref, v_ref, o_ref, lse_ref, qd_ref, kd_ref, va_ref, *, seq, rate):
    qb = DIL_QBLOCK
    cpb = seq // rate // qb
    shift = cpb.bit_length() - 1
    lane, head, aux = _head_lanes((qb, LANES))

    def token_rows(n):
        if rate == 1:
            return pl.ds(pl.multiple_of(n * qb, qb), qb)
        c, ch = lax.shift_right_logical(n, shift), n & (cpb - 1)
        return pl.ds(c + ch * (qb * rate), qb, stride=rate)

    def prep(n):
        src = token_rows(n)
        dst = pl.ds(pl.multiple_of(n * qb, qb), qb)
        kn = k_ref[0, src, :]
        v = v_ref[0, src, :]
        qd_ref[dst, :] = (q_ref[0, src, :] * QSCALE).astype(BF16)
        for h in range(2):
            kd_ref[h, dst, :] = jnp.where(head[h], kn, 0.0).astype(BF16)
            va_ref[h, dst, :] = jnp.concatenate(
                [jnp.where(head[h], v, 0.0), jnp.where(head[h], 1.0, 0.0)], axis=1).astype(BF16)

    def unrolled(fn):
        def step(it, carry):
            for u in range(DIL_UNROLL):
                fn(it * DIL_UNROLL + u)
            return carry
        lax.fori_loop(0, seq // qb // DIL_UNROLL, step, 0)

    unrolled(prep)

    rel = (lax.broadcasted_iota(jnp.int32, (qb, 2 * qb), 0)
           - lax.broadcasted_iota(jnp.int32, (qb, 2 * qb), 1))
    nt = (((1,), (1,)), ((), ()))

    def qblock(n):
        first = (n & (cpb - 1)) == 0
        k0 = jnp.where(first, n, n - 1) * qb
        krows = pl.ds(pl.multiple_of(k0, qb), 2 * qb)
        qrows = pl.ds(pl.multiple_of(n * qb, qb), qb)
        dist = rel + (n * qb - k0)
        ok = (dist >= 0) & (dist <= DIL_WINDOW)
        keys = jnp.concatenate([kd_ref[0, krows, :], kd_ref[1, krows, :]], axis=0)
        s = lax.dot_general(qd_ref[qrows, :], keys, nt, preferred_element_type=F32)
        ms, ps = [], []
        for h in range(2):
            sh = jnp.where(ok, s[:, h * 2 * qb:(h + 1) * 2 * qb], NEG_INF)
            ms.append(jnp.max(sh, axis=-1, keepdims=True))
            ps.append(jnp.exp2(sh - ms[h]).astype(BF16))
        vals = jnp.concatenate([va_ref[0, krows, :], va_ref[1, krows, :]], axis=0)
        acc = jnp.dot(jnp.concatenate(ps, axis=1), vals, preferred_element_type=F32)
        den = acc[:, LANES:]
        dst = token_rows(n)
        o_ref[0, dst, :] = acc[:, :LANES] / den
        lse_ref[0, dst, :] = jnp.where(head[0], ms[0], ms[1]) + jnp.log2(den)

    unrolled(qblock)


def dilated_group(att3, group):
    bsz, seq, cols = att3.shape
    rate = DIL_RATES[group]
    cpb = seq // rate // DIL_QBLOCK
    assert cpb >= 2 and cpb & (cpb - 1) == 0 and (seq // DIL_QBLOCK) % DIL_UNROLL == 0
    hp = DIL_WIDTH // LANES
    base = COL_DIL // LANES
    qkv = lambda part: pl.BlockSpec(
        (1, seq, LANES), lambda b, p, part=part: (b, 0, base + (part * DIL_GROUPS + group) * hp + p))
    out = pl.BlockSpec((1, seq, LANES), lambda b, p: (b, 0, p))
    shp = jax.ShapeDtypeStruct((bsz, seq, DIL_WIDTH), F32)
    return pl.pallas_call(
        functools.partial(_dilated_kernel, seq=seq, rate=rate), out_shape=(shp, shp), grid=(bsz, hp),
        in_specs=[qkv(0), qkv(1), qkv(2)],
        out_specs=(out, out),
        scratch_shapes=[pltpu.VMEM((seq, LANES), BF16), pltpu.VMEM((2, seq, LANES), BF16),
                        pltpu.VMEM((2, seq, 2 * LANES), BF16)],
        compiler_params=_params("parallel", "parallel"), name=f"dilated_rate{rate}",
    )(att3, att3, att3)


def _ssd_kernel(*refs):
    xpx_ref, xpb_ref, st_ref = refs[-3:]

    @pl.when(pl.program_id(1) == 0)
    def _():
        xpx_ref[0:SSD_PAD, :] = jnp.zeros((SSD_PAD, SSM_INNER), F32)
        xpb_ref[0:SSD_PAD, :] = jnp.zeros((SSD_PAD, SSM_INNER), F32)
        st_ref[...] = jnp.zeros(st_ref.shape, F32)

    for c in range(SSD_STEP_CHUNKS):
        _ssd_chunk(slice(c * SSM_CHUNK, (c + 1) * SSM_CHUNK), *refs)


def _ssd_chunk(rows, z_ref, xs_ref, bc_ref, dt_ref, cwx_ref, cwb_ref, cbx_ref, cbb_ref, dtb_ref, alog_ref,
               dexp_ref, onorm_ref, tri_ref, triu_ref, exp_ref, o_ref, xpx_ref, xpb_ref, st_ref):
    L = SSM_CHUNK
    pad = SSD_PAD

    def conv_silu(src_ref, pad_ref, w_ref, b_ref):
        pad_ref[pad:, :] = src_ref[0, rows, :].astype(F32)
        acc = b_ref[...] + w_ref[0:1, :] * pad_ref[pl.ds(pad - SSM_CONV + 1, L), :]
        for k in range(1, SSM_CONV):
            acc = acc + w_ref[k:k + 1, :] * pad_ref[pl.ds(pad - SSM_CONV + 1 + k, L), :]
        pad_ref[0:pad, :] = pad_ref[L:L + pad, :]
        return _silu(acc)

    xs = conv_silu(xs_ref, xpx_ref, cwx_ref, cbx_ref)
    bc = conv_silu(bc_ref, xpb_ref, cwb_ref, cbb_ref)
    gn = SSM_GROUPS * SSM_STATE
    bm, cm = bc[:, :gn], bc[:, gn:]

    xr = dt_ref[0, rows, :] + dtb_ref[...]
    dt = jnp.maximum(xr, 0.0) + jnp.log(1.0 + jnp.exp(-jnp.abs(xr)))
    adt = dt * (-jnp.exp(alog_ref[...]) * math.log2(math.e))
    acs = _dot_exact_lhs(tri_ref[...], adt)
    acs_t = _dot_exact_rhs(adt.T, triu_ref[...])
    expand = exp_ref[...]
    dt_e = _dot_exact_rhs(dt, expand)
    acs_e = _dot_exact_rhs(acs, expand)
    xdt = xs * dt_e
    last = acs_e[L - 1:L, :]
    grow = jnp.exp2(acs_e)
    to_end = jnp.exp2(last - acs_e)
    chunk_decay = jnp.exp2(last)

    ll = lax.broadcasted_iota(jnp.int32, (L, L), 0)
    ss = lax.broadcasted_iota(jnp.int32, (L, L), 1)
    causal = ll >= ss
    gw = SSM_INNER // SSM_GROUPS
    hpg = SSM_HEADS // SSM_GROUPS
    lane = lax.broadcasted_iota(jnp.int32, (L, gw), 1)
    nt = (((1,), (1,)), ((), ()))
    tn = (((0,), (0,)), ((), ()))
    ys = []
    for g in range(SSM_GROUPS):
        cols = slice(g * gw, (g + 1) * gw)
        bg = bm[:, g * SSM_STATE:(g + 1) * SSM_STATE].astype(BF16)
        cg = cm[:, g * SSM_STATE:(g + 1) * SSM_STATE].astype(BF16)
        xg = xdt[:, cols]
        cb = lax.dot_general(cg, bg, nt, preferred_element_type=F32)
        st = st_ref[g]
        y = jnp.dot(cg, st.astype(BF16), preferred_element_type=F32) * grow[:, cols]
        new = lax.dot_general(bg, (xg * to_end[:, cols]).astype(BF16), tn, preferred_element_type=F32)
        st_ref[g] = chunk_decay[:, cols] * st + new
        for hh in range(hpg):
            h = g * hpg + hh
            diff = acs[:, h:h + 1] - acs_t[h:h + 1, :]
            mat = (cb * jnp.exp2(jnp.where(causal, diff, -jnp.inf))).astype(BF16)
            xh = jnp.where((lane >= hh * SSM_HEAD_DIM) & (lane < (hh + 1) * SSM_HEAD_DIM), xg, 0.0)
            y = y + jnp.dot(mat, xh.astype(BF16), preferred_element_type=F32)
        ys.append(y)
    y = jnp.concatenate(ys, axis=1) + xs * dexp_ref[...]
    yg = y * _silu(z_ref[0, rows, :].astype(F32))
    ms = jnp.mean(yg * yg, axis=-1, keepdims=True)
    o_ref[0, rows, :] = yg * lax.rsqrt(ms + NORM_EPS) * onorm_ref[...]


def ssd_mixer(mix3, dt3, conv_w, conv_b, dt_bias, a_log, d_exp, out_norm, consts):
    bsz, seq, _ = mix3.shape
    L = SSM_CHUNK
    w = SSM_INNER
    tri, triu, expand = consts
    step = SSD_STEP_CHUNKS * L if seq % (SSD_STEP_CHUNKS * L) == 0 else None
    assert step is not None, "sequence must hold a whole number of grid steps"
    col = lambda idx: pl.BlockSpec((1, step, w), lambda b, c, idx=idx: (b, c, idx))
    vecw = lambda rows, idx: pl.BlockSpec((rows, w), lambda b, c, idx=idx: (0, idx))
    vec = pl.BlockSpec((1, LANES), lambda b, c: (0, 0))
    sq = pl.BlockSpec((L, L), lambda b, c: (0, 0))
    return pl.pallas_call(
        _ssd_kernel, out_shape=jax.ShapeDtypeStruct((bsz, seq, w), F32), grid=(bsz, seq // step),
        in_specs=[col(MIX_Z // w), col(MIX_XBC // w), col(MIX_XBC // w + 1),
                  pl.BlockSpec((1, step, LANES), lambda b, c: (b, c, 0)),
                  vecw(SSM_CONV, 0), vecw(SSM_CONV, 1), vecw(1, 0), vecw(1, 1), vec, vec,
                  vecw(1, 0), vecw(1, 0), sq, sq, pl.BlockSpec((LANES, w), lambda b, c: (0, 0))],
        out_specs=pl.BlockSpec((1, step, w), lambda b, c: (b, c, 0)),
        scratch_shapes=[pltpu.VMEM((L + SSD_PAD, w), F32), pltpu.VMEM((L + SSD_PAD, w), F32),
                        pltpu.VMEM((SSM_GROUPS, SSM_STATE, w // SSM_GROUPS), F32)],
        compiler_params=_params("parallel", "arbitrary"), name="ssd_mixer",
    )(mix3, mix3, mix3, dt3, conv_w, conv_w, conv_b, conv_b, dt_bias, a_log, d_exp, out_norm,
      tri, triu, expand)


def _merge_kernel(x_ref, a_ref, m_ref, o0_ref, o1_ref, o2_ref, l0_ref, l1_ref, l2_ref, gl_ref, bg_ref,
                  wa_ref, wm_ref, wc_ref, wo_ref, out_ref):
    l0, l1, l2 = l0_ref[...], l1_ref[...], l2_ref[...]
    lmax = jnp.maximum(jnp.maximum(l0, l1), l2)
    e0, e1, e2 = jnp.exp2(l0 - lmax), jnp.exp2(l1 - lmax), jnp.exp2(l2 - lmax)
    cmix = (e0 * o0_ref[...] + e1 * o1_ref[...] + e2 * o2_ref[...]) / (e0 + e1 + e2)
    gates = _sigmoid(gl_ref[...] + bg_ref[...])
    d = D_MODEL
    mm = lambda v, w_ref: jnp.dot(v.astype(BF16), w_ref[...], preferred_element_type=F32)
    merged = (gates[:, :d] * mm(a_ref[...], wa_ref) + gates[:, d:2 * d] * mm(m_ref[...], wm_ref)
              + gates[:, 2 * d:] * mm(cmix, wc_ref))
    out_ref[...] = x_ref[...] + mm(merged, wo_ref)


def merge_branches(x2, a2, m2, dil, mix2, b_gate, wa, wm, wc, wo, tm=512):
    n, d = x2.shape
    tm = min(tm, n)
    row = lambda width, idx=0: pl.BlockSpec((tm, width), lambda i, idx=idx: (i, idx))
    full = lambda arr: pl.BlockSpec(arr.shape, lambda i: (0, 0), pipeline_mode=pl.Buffered(1))
    (o0, l0), (o1, l1), (o2, l2) = dil
    gw = N_BRANCH * d
    return pl.pallas_call(
        _merge_kernel, out_shape=jax.ShapeDtypeStruct((n, d), F32), grid=(n // tm,),
        in_specs=[row(d), row(MOBA_WIDTH), row(SSM_INNER)] + [row(DIL_WIDTH)] * 6
                 + [row(gw, MIX_GATE // gw), full(b_gate), full(wa), full(wm), full(wc), full(wo)],
        out_specs=row(d), compiler_params=_params("parallel"), name="merge_branches",
    )(x2, a2, m2, o0, o1, o2, l0, l1, l2, mix2, b_gate, wa, wm, wc, wo)


def _rms_bf16(x, gain):
    ms = jnp.mean(x * x, axis=-1, keepdims=True)
    return (x * lax.rsqrt(ms + NORM_EPS) * gain).astype(BF16)


def _ffn_ple_kernel(x_ref, xh_ref, p_ref, gf_ref, wup_ref, cw_ref, cb_ref, wd_ref, gp_ref, wg_ref, wp_ref,
                    o_ref, u_ref, buf_ref, h_ref, *, tm, tiles_per_seq):
    ck = FFN_CHUNK
    nck = FFN_DIM // ck
    u_ref[0:HALO, :] = _rms_bf16(xh_ref[...], gf_ref[...])
    u_ref[HALO:, :] = _rms_bf16(x_ref[...], gf_ref[...])
    keep = jnp.where(pl.program_id(0) % tiles_per_seq == 0, 0.0, 1.0)

    def up(c):
        u = u_ref[...]
        for half in range(2):
            buf = buf_ref.at[2 * (c % 2) + half]
            cols = slice(half * FFN_DIM + c * ck, half * FFN_DIM + (c + 1) * ck)
            val = jnp.dot(u, wup_ref[:, cols], preferred_element_type=F32)
            buf[0:HALO, :] = val[0:HALO, :] * keep
            buf[HALO:, :] = val[HALO:, :]

    def conv(c, half):
        buf = buf_ref.at[2 * (c % 2) + half]
        cols = slice(half * FFN_DIM + c * ck, half * FFN_DIM + (c + 1) * ck)
        acc = cb_ref[:, cols] + cw_ref[0:1, cols] * buf[pl.ds(HALO - FFN_CONV + 1, tm), :]
        for k in range(1, FFN_CONV):
            acc = acc + cw_ref[k:k + 1, cols] * buf[pl.ds(HALO - FFN_CONV + 1 + k, tm), :]
        return acc

    up(0)
    for c in range(nck):
        if c + 1 < nck:
            up(c + 1)
        h_ref[:, c * ck:(c + 1) * ck] = (_silu(conv(c, 0)) * conv(c, 1)).astype(BF16)
    acc = x_ref[...] + jnp.dot(h_ref[...], wd_ref[...], preferred_element_type=F32)
    pg = _sigmoid(jnp.dot(_rms_bf16(acc, gp_ref[...]), wg_ref[...], preferred_element_type=F32))
    o_ref[...] = acc + jnp.dot(p_ref[...].astype(BF16), wp_ref[...], preferred_element_type=F32) * pg


def ffn_ple(x2, p_all, layer, gain_ffn, w_up, conv_w, conv_b, w_down, gain_ple, w_gate, w_ple, seq, tm=512):
    n, d = x2.shape
    tm = min(tm, seq)
    hb = tm // HALO
    row = lambda width: pl.BlockSpec((tm, width), lambda i: (i, 0))
    p_rows = pl.BlockSpec((tm, PLE_DIM), lambda i: (layer * (n // tm) + i, 0))
    resident = lambda arr: pl.BlockSpec(arr.shape, lambda i: (0, 0), pipeline_mode=pl.Buffered(1))
    return pl.pallas_call(
        functools.partial(_ffn_ple_kernel, tm=tm, tiles_per_seq=seq // tm),
        out_shape=jax.ShapeDtypeStruct((n, d), F32), grid=(n // tm,),
        in_specs=[row(d), pl.BlockSpec((HALO, d), lambda i: (jnp.maximum(i * hb - 1, 0), 0)), p_rows,
                  resident(gain_ffn), resident(w_up), resident(conv_w), resident(conv_b), resident(w_down),
                  resident(gain_ple), resident(w_gate), resident(w_ple)],
        out_specs=row(d),
        scratch_shapes=[pltpu.VMEM((tm + HALO, d), BF16), pltpu.VMEM((4, tm + HALO, FFN_CHUNK), F32),
                        pltpu.VMEM((tm, FFN_DIM), BF16)],
        compiler_params=_params("parallel"), name="ffn_ple",
    )(x2, x2, p_all, gain_ffn, w_up, conv_w, conv_b, w_down, gain_ple, w_gate, w_ple)


def _constants():
    lane = np.arange(LANES)
    bd = (lane[:, None] // HEAD_DIM == lane[None, :] // HEAD_DIM).astype(np.float32) / HEAD_DIM
    r = np.arange(SSM_CHUNK)
    tri = (r[None, :] <= r[:, None]).astype(np.float32)
    expand = (lane[:, None] == (np.arange(SSM_INNER)[None, :] // SSM_HEAD_DIM)).astype(np.float32)
    as_bf16 = lambda a: jnp.asarray(a, dtype=BF16)
    return as_bf16(bd), (as_bf16(tri), as_bf16(tri.T), as_bf16(expand))


def _pad_lanes(v):
    return jnp.pad(v, (0, LANES - v.shape[0]))[None, :]


def kernel(x, p, positions, norm_mix, w_in, b_gate, moba_q_norm, moba_k_norm, dil_q_norm, dil_k_norm,
           ssm_conv_w, ssm_conv_b, ssm_dt_bias, ssm_a_log, ssm_d, ssm_out_norm, w_br_moba, w_br_ssm,
           w_br_dil, w_out, norm_ffn, w_up, ffn_conv_w, ffn_conv_b, w_down, norm_ple, w_ple_gate, w_ple):
    bsz, seq, d = x.shape
    depth = w_in.shape[0]
    n = bsz * seq
    bd, ssd_consts = _constants()
    tables = [t.reshape(n, LANES) for t in rope_tables(positions)]
    row = lambda v: v[None, :]
    heads = lambda v, count: jnp.tile(v, count)
    ones = lambda count: jnp.ones((count,), F32)

    x2 = x.reshape(n, d)
    for i in range(depth):
        w_all = w_in[i].astype(BF16)
        w_dt = jnp.pad(w_all[:, COL_DT:COL_GATE], ((0, 0), (0, LANES - SSM_HEADS)))
        half = ROPE_DIM // 2
        head_gain = jnp.stack([jnp.concatenate([
            heads(jnp.roll(moba_q_norm[i], s), MOBA_HEADS), heads(jnp.roll(moba_k_norm[i], s), MOBA_HEADS),
            ones(MOBA_WIDTH), heads(jnp.roll(dil_q_norm[i], s), DIL_GROUPS * DIL_HEADS),
            heads(jnp.roll(dil_k_norm[i], s), DIL_GROUPS * DIL_HEADS), ones(DIL_GROUPS * DIL_WIDTH)])
            for s in (0, half, -half)])
        att2, mix2, dt2 = in_projection(x2, row(norm_mix[i]), w_all, w_all[:, COL_GATE:], w_dt,
                                        head_gain, bd, tables)
        att3 = att2.reshape(bsz, seq, COL_Z)

        out_a = moba_attention(att3)
        out_b = ssd_mixer(mix2.reshape(bsz, seq, MIX_COLS), dt2.reshape(bsz, seq, LANES), ssm_conv_w[i],
                          row(ssm_conv_b[i]), _pad_lanes(ssm_dt_bias[i]), _pad_lanes(ssm_a_log[i]),
                          row(jnp.repeat(ssm_d[i], SSM_HEAD_DIM)), row(ssm_out_norm[i]), ssd_consts)
        dil = [dilated_group(att3, g) for g in range(DIL_GROUPS)]
        dil2 = [(o.reshape(n, DIL_WIDTH), l.reshape(n, DIL_WIDTH)) for o, l in dil]

        x2 = merge_branches(x2, out_a.reshape(n, MOBA_WIDTH), out_b.reshape(n, SSM_INNER), dil2, mix2,
                            row(b_gate[i]), w_br_moba[i].astype(BF16), w_br_ssm[i].astype(BF16),
                            w_br_dil[i].astype(BF16), w_out[i].astype(BF16))
        x2 = ffn_ple(x2, p.reshape(depth * n, PLE_DIM), i, row(norm_ffn[i]), w_up[i].astype(BF16), ffn_conv_w[i],
                     row(ffn_conv_b[i]), w_down[i].astype(BF16), row(norm_ple[i]),
                     w_ple_gate[i].astype(BF16), w_ple[i].astype(BF16), seq)
    return x2.reshape(bsz, seq, d)
```

```python
import functools
import math

import numpy as np
import jax
import jax.numpy as jnp
from jax import lax
from jax.experimental import pallas as pl
from jax.experimental.pallas import tpu as pltpu

F32 = jnp.float32
BF16 = jnp.bfloat16

D_MODEL = 1024
PLE_DIM = 256
HEAD_DIM = 64
ROPE_DIM = HEAD_DIM // 4
ROPE_THETA = 500000.0
NORM_EPS = 1e-6
NEG_INF = -1e30

MOBA_HEADS = 8
MOBA_BLOCK = 256
MOBA_TOPK = 3
MOBA_WIDTH = MOBA_HEADS * HEAD_DIM

DIL_RATES = (1, 4, 16)
DIL_GROUPS = 3
DIL_HEADS = 8
DIL_WINDOW = 128
DIL_WIDTH = DIL_HEADS * HEAD_DIM
DIL_QBLOCK = 128
DIL_UNROLL = 4

SSM_INNER = D_MODEL
SSM_HEAD_DIM = 64
SSM_HEADS = SSM_INNER // SSM_HEAD_DIM
SSM_GROUPS = 4
SSM_STATE = 128
SSM_CONV = 4
SSM_CHUNK = 128
SSM_XBC = SSM_INNER + 2 * SSM_GROUPS * SSM_STATE
SSD_STEP_CHUNKS = 4
SSD_PAD = 8

FFN_DIM = 2816
FFN_CONV = 3
FFN_CHUNK = 256
N_BRANCH = 3

COL_MOBA = 0
COL_DIL = 3 * MOBA_WIDTH
COL_Z = COL_DIL + 3 * DIL_GROUPS * DIL_WIDTH
COL_XBC = COL_Z + SSM_INNER
COL_DT = COL_XBC + SSM_XBC
COL_GATE = COL_DT + SSM_HEADS
IN_COLS = COL_GATE + N_BRANCH * D_MODEL
IN_TILE = 1536
QK_ROWS = 256
ATT_TILES = COL_Z // IN_TILE
MAIN_TILES = COL_DT // IN_TILE
GATE_TILES = N_BRANCH * D_MODEL // IN_TILE
MIX_Z = 0
MIX_XBC = SSM_INNER
MIX_GATE = SSM_INNER + SSM_XBC
MIX_COLS = MIX_GATE + N_BRANCH * D_MODEL

QSCALE = HEAD_DIM ** -0.5 * math.log2(math.e)

LANES = 128
HALO = 16
VMEM_LIMIT = 56 * 1024 * 1024


def _params(*sem):
    return pltpu.CompilerParams(dimension_semantics=sem, vmem_limit_bytes=VMEM_LIMIT)


def _sigmoid(x):
    return 1.0 / (1.0 + jnp.exp2(x * -math.log2(math.e)))


def _silu(x):
    return x * _sigmoid(x)


def _split3(a):
    a1 = a.astype(BF16)
    r1 = a - a1.astype(F32)
    a2 = r1.astype(BF16)
    a3 = (r1 - a2.astype(F32)).astype(BF16)
    return a1, a2, a3


def _dot_exact_rhs(a, b_exact, passes=3):
    out = None
    for piece in _split3(a)[:passes]:
        t = jnp.dot(piece, b_exact, preferred_element_type=F32)
        out = t if out is None else out + t
    return out


def _dot_exact_lhs(a_exact, b, passes=3):
    out = None
    for piece in _split3(b)[:passes]:
        t = jnp.dot(a_exact, piece, preferred_element_type=F32)
        out = t if out is None else out + t
    return out


def _head_norm_rope(x, bd, gcos, gup, gdn):
    ms = jnp.dot((x * x).astype(BF16), bd, preferred_element_type=F32)
    half = ROPE_DIM // 2
    rot = x * gcos + pltpu.roll(x, half, 1) * gup + pltpu.roll(x, LANES - half, 1) * gdn
    return rot * lax.rsqrt(ms + NORM_EPS)


def _rope_kernel(pos_ref, inv_ref, cos_ref, up_ref, dn_ref):
    ang = pos_ref[0] * inv_ref[...]
    d = lax.broadcasted_iota(jnp.int32, ang.shape, 1) % HEAD_DIM
    half = ROPE_DIM // 2
    s = jnp.sin(ang)
    cos_ref[0] = jnp.cos(ang)
    up_ref[0] = jnp.where((d >= half) & (d < ROPE_DIM), s, 0.0)
    dn_ref[0] = jnp.where(d < half, -s, 0.0)


def rope_tables(positions):
    bsz, seq = positions.shape
    ts = min(seq, 1024)
    d = np.arange(LANES) % HEAD_DIM
    inv = ROPE_THETA ** (-jnp.arange(0, ROPE_DIM, 2, dtype=F32) / ROPE_DIM)
    inv_lane = jnp.where(d < ROPE_DIM, inv[d % (ROPE_DIM // 2)], 0.0).astype(F32)[None, :]
    pos = positions.astype(F32)[..., None]
    shp = jax.ShapeDtypeStruct((bsz, seq, LANES), F32)
    spec = pl.BlockSpec((1, ts, LANES), lambda b, t: (b, t, 0))
    return pl.pallas_call(
        _rope_kernel, out_shape=(shp, shp, shp), grid=(bsz, seq // ts),
        in_specs=[pl.BlockSpec((1, ts, 1), lambda b, t: (b, t, 0)),
                  pl.BlockSpec((1, LANES), lambda b, t: (0, 0))],
        out_specs=(spec, spec, spec), compiler_params=_params("parallel", "parallel"),
        name="rope_tables")(pos, inv_lane)


def _inproj_kernel(x_ref, g_ref, w_ref, wg_ref, wdt_ref, hg_ref, bd_ref, cos_ref, up_ref, dn_ref,
                   att_ref, mix_ref, dt_ref, u_ref, *, tm):
    j = pl.program_id(1)

    @pl.when(j == 0)
    def _():
        u = _rms_bf16(x_ref[...], g_ref[...])
        u_ref[...] = u
        dt_ref[...] = jnp.dot(u, wdt_ref[...], preferred_element_type=F32)

    def attention_tile(gain_sets):
        att_ref[...] = jnp.dot(u_ref[...], w_ref[...], preferred_element_type=F32)
        bd = bd_ref[...]
        for r0 in range(0, tm, QK_ROWS):
            rows = slice(r0, r0 + QK_ROWS)
            for first, count in gain_sets:
                g0 = slice(first * LANES, (first + 1) * LANES)
                gcos = hg_ref[0:1, g0] * cos_ref[rows, :]
                gup = hg_ref[1:2, g0] * up_ref[rows, :]
                gdn = hg_ref[2:3, g0] * dn_ref[rows, :]
                for g in range(first, first + count):
                    lanes = slice(g * LANES, (g + 1) * LANES)
                    att_ref[rows, lanes] = _head_norm_rope(att_ref[rows, lanes], bd, gcos, gup, gdn)

    per_part = MOBA_WIDTH // LANES

    @pl.when(j == 0)
    def _():
        attention_tile([(0, per_part), (per_part, per_part)])

    @pl.when((j == 1) | (j == 2))
    def _():
        attention_tile([(0, IN_TILE // LANES)])

    @pl.when(j == 3)
    def _():
        attention_tile([])

    @pl.when((j >= ATT_TILES) & (j < MAIN_TILES))
    def _():
        mix_ref[...] = jnp.dot(u_ref[...], w_ref[...], preferred_element_type=F32).astype(BF16)

    @pl.when(j >= MAIN_TILES)
    def _():
        mix_ref[...] = jnp.dot(u_ref[...], wg_ref[...], preferred_element_type=F32).astype(BF16)


def in_projection(x2, gain, w_all, w_gate, w_dt, head_gain, bd, tables, tm=1024):
    n, d = x2.shape
    tm = min(tm, n)
    tn = IN_TILE
    assert 3 * MOBA_WIDTH == tn and DIL_GROUPS * DIL_WIDTH == tn and tm % QK_ROWS == 0
    tab = pl.BlockSpec((tm, LANES), lambda i, j: (i, 0))
    return pl.pallas_call(
        functools.partial(_inproj_kernel, tm=tm),
        out_shape=(jax.ShapeDtypeStruct((n, COL_Z), F32), jax.ShapeDtypeStruct((n, MIX_COLS), BF16),
                   jax.ShapeDtypeStruct((n, LANES), F32)),
        grid=(n // tm, MAIN_TILES + GATE_TILES),
        in_specs=[pl.BlockSpec((tm, d), lambda i, j: (i, 0)),
                  pl.BlockSpec((1, d), lambda i, j: (0, 0)),
                  pl.BlockSpec((d, tn), lambda i, j: (0, jnp.minimum(j, MAIN_TILES - 1))),
                  pl.BlockSpec((d, tn), lambda i, j: (0, jnp.maximum(j - MAIN_TILES, 0))),
                  pl.BlockSpec((d, LANES), lambda i, j: (0, 0)),
                  pl.BlockSpec((3, tn), lambda i, j: (0, jnp.minimum(j, ATT_TILES - 1))),
                  pl.BlockSpec((LANES, LANES), lambda i, j: (0, 0)), tab, tab, tab],
        out_specs=(pl.BlockSpec((tm, tn), lambda i, j: (i, jnp.minimum(j, ATT_TILES - 1))),
                   pl.BlockSpec((tm, tn), lambda i, j: (i, jnp.maximum(j - ATT_TILES, 0))),
                   pl.BlockSpec((tm, LANES), lambda i, j: (i, 0))),
        scratch_shapes=[pltpu.VMEM((tm, d), BF16)],
        compiler_params=_params("parallel", "arbitrary"), name="in_projection",
    )(x2, gain, w_all, w_gate, w_dt, head_gain, bd, *tables)


def _head_lanes(shape):
    lane = lax.broadcasted_iota(jnp.int32, shape, len(shape) - 1)
    return lane, (lane < HEAD_DIM, lane >= HEAD_DIM), (HEAD_DIM, 0)


def _moba_kernel(q_ref, k_ref, v_ref, o_ref, qa_ref, ka_ref, va_ref, km_ref, *, nb):
    blk = MOBA_BLOCK
    lane, head, aux = _head_lanes((blk, LANES))
    nbp = km_ref.shape[0]
    km_ref[...] = jnp.zeros(km_ref.shape, F32)

    def prep(j):
        rows = pl.ds(pl.multiple_of(j * blk, blk), blk)
        kn = k_ref[0, rows, :]
        v = v_ref[0, rows, :]
        km_ref[pl.ds(j, 1), :] = jnp.mean(kn, axis=0, keepdims=True)
        for h in range(2):
            ka_ref[h, rows, :] = jnp.where(head[h], kn, jnp.where(lane == aux[h] + j, 1.0, 0.0)).astype(BF16)
            va_ref[h, rows, :] = jnp.where(head[h], v, jnp.where(lane == aux[h], 1.0, 0.0)).astype(BF16)

    def prep2(t, carry):
        prep(2 * t)
        prep(2 * t + 1)
        return carry

    lax.fori_loop(0, nb // 2, prep2, 0)

    bidx = lax.broadcasted_iota(jnp.int32, (nbp, blk), 0).astype(F32)
    isblk = bidx < float(nb)
    _, head_k, _ = _head_lanes((nbp, LANES))
    nt = (((1,), (1,)), ((), ()))

    def select(i):
        rows = pl.ds(pl.multiple_of(i * blk, blk), blk)
        qf = q_ref[0, rows, :]
        q_hi = qf.astype(BF16)
        q_lo = (qf - q_hi.astype(F32)).astype(BF16)
        i_f = lax.convert_element_type(i, F32)
        for h in range(2):
            km = jnp.where(head_k[h], km_ref[...], 0.0)
            k_hi = km.astype(BF16)
            k_lo = (km - k_hi.astype(F32)).astype(BF16)
            dot = lambda a, b: lax.dot_general(a, b, nt, preferred_element_type=F32)
            sc = dot(k_hi, q_hi) + (dot(k_hi, q_lo) + dot(k_lo, q_hi))
            valid = isblk & (bidx < i_f)
            cur = jnp.where(valid, sc, -jnp.inf)
            sel = jnp.zeros((nbp, blk), F32)
            for _ in range(min(MOBA_TOPK, nb)):
                mx = jnp.max(cur, axis=0, keepdims=True)
                first = jnp.min(jnp.where((cur == mx) & isblk, bidx, float(nbp)), axis=0, keepdims=True)
                hit = bidx == first
                sel = jnp.where(hit, 1.0, sel)
                cur = jnp.where(hit, -jnp.inf, cur)
            keep = ((sel > 0.5) & valid) | (bidx == i_f)
            bias = jnp.where(isblk & jnp.logical_not(keep), NEG_INF, 0.0)
            pieces = [jnp.zeros((aux[h], blk), F32)] if aux[h] else []
            pieces += [bias, jnp.zeros((LANES - aux[h] - nbp, blk), F32)]
            bias_t = jnp.concatenate(pieces, axis=0).T
            qa_ref[h, rows, :] = jnp.where(head[h], qf * QSCALE, bias_t).astype(BF16)

    per_step = 4 if nb % 4 == 0 else 2

    def select_step(t, carry):
        for u in range(per_step):
            select(per_step * t + u)
        return carry

    lax.fori_loop(0, nb // per_step, select_step, 0)

    wide = 2 * blk
    _, head_w, _ = _head_lanes((wide, LANES))
    causal = (lax.broadcasted_iota(jnp.int32, (wide, wide), 1)
              <= lax.broadcasted_iota(jnp.int32, (wide, wide), 0))
    nt = (((1,), (1,)), ((), ()))

    def rows_of(t):
        return slice(t * wide, (t + 1) * wide)

    def logits(a, g):
        return [lax.dot_general(qa_ref[h, rows_of(a), :], ka_ref[h, rows_of(g), :], nt,
                                preferred_element_type=F32) for h in range(2)]

    tiles = [(a, g) for a in range(nb // 2) for g in [a] + list(range(a))]
    ss = logits(*tiles[0])
    state = None
    for t, (a, g) in enumerate(tiles):
        nxt = logits(*tiles[t + 1]) if t + 1 < len(tiles) else None
        new = []
        for h in range(2):
            s = ss[h]
            if g == a:
                s = jnp.where(causal, s, NEG_INF)
                m = jnp.max(s, axis=-1, keepdims=True)
                acc = jnp.dot(jnp.exp2(s - m).astype(BF16), va_ref[h, rows_of(g), :],
                              preferred_element_type=F32)
            else:
                m_old, acc_old = state[h]
                m = jnp.maximum(m_old, jnp.max(s, axis=-1, keepdims=True))
                acc = jnp.exp2(m_old - m) * acc_old + jnp.dot(
                    jnp.exp2(s - m).astype(BF16), va_ref[h, rows_of(g), :], preferred_element_type=F32)
            new.append((m, acc))
        state, ss = new, nxt
        if g == a - 1 or a == 0:
            den0 = state[0][1][:, aux[0]:aux[0] + 1]
            den1 = state[1][1][:, aux[1]:aux[1] + 1]
            o_ref[0, rows_of(a), :] = jnp.where(head_w[0], state[0][1] / den0, state[1][1] / den1)


def moba_attention(att3):
    bsz, seq, _ = att3.shape
    nb = seq // MOBA_BLOCK
    assert nb % 2 == 0 and nb <= HEAD_DIM, "key blocks are visited in pairs and indexed on 64 spare lanes"
    hp = MOBA_WIDTH // LANES
    qkv = lambda part: pl.BlockSpec((1, seq, LANES), lambda b, p, part=part: (b, 0, part * hp + p))
    return pl.pallas_call(
        functools.partial(_moba_kernel, nb=nb),
        out_shape=jax.ShapeDtypeStruct((bsz, seq, MOBA_WIDTH), F32), grid=(bsz, hp),
        in_specs=[qkv(0), qkv(1), qkv(2)],
        out_specs=pl.BlockSpec((1, seq, LANES), lambda b, p: (b, 0, p)),
        scratch_shapes=[pltpu.VMEM((2, seq, LANES), BF16), pltpu.VMEM((2, seq, LANES), BF16),
                        pltpu.VMEM((2, seq, LANES), BF16), pltpu.VMEM((-(-nb // 8) * 8, LANES), F32)],
        compiler_params=_params("parallel", "parallel"), name="moba_attention",
    )(att3, att3, att3)


def _dilated_kernel(q_ref, k_ref, v_ref, o_ref, lse_ref, qd_ref, kd_ref, va_ref, *, seq, rate):
    qb = DIL_QBLOCK
    cpb = seq // rate // qb
    shift = cpb.bit_length() - 1
    lane, head, aux = _head_lanes((qb, LANES))

    def token_rows(n):
        if rate == 1:
            return pl.ds(pl.multiple_of(n * qb, qb), qb)
        c, ch = lax.shift_right_logical(n, shift), n & (cpb - 1)
        return pl.ds(c + ch * (qb * rate), qb, stride=rate)

    def prep(n):
        src = token_rows(n)
        dst = pl.ds(pl.multiple_of(n * qb, qb), qb)
        kn = k_ref[0, src, :]
        v = v_ref[0, src, :]
        qd_ref[dst, :] = (q_ref[0, src, :] * QSCALE).astype(BF16)
        for h in range(2):
            kd_ref[h, dst, :] = jnp.where(head[h], kn, 0.0).astype(BF16)
            va_ref[h, dst, :] = jnp.concatenate(
                [jnp.where(head[h], v, 0.0), jnp.where(head[h], 1.0, 0.0)], axis=1).astype(BF16)

    def unrolled(fn):
        def step(it, carry):
            for u in range(DIL_UNROLL):
                fn(it * DIL_UNROLL + u)
            return carry
        lax.fori_loop(0, seq // qb // DIL_UNROLL, step, 0)

    unrolled(prep)

    rel = (lax.broadcasted_iota(jnp.int32, (qb, 2 * qb), 0)
           - lax.broadcasted_iota(jnp.int32, (qb, 2 * qb), 1))
    nt = (((1,), (1,)), ((), ()))

    def qblock(n):
        first = (n & (cpb - 1)) == 0
        k0 = jnp.where(first, n, n - 1) * qb
        krows = pl.ds(pl.multiple_of(k0, qb), 2 * qb)
        qrows = pl.ds(pl.multiple_of(n * qb, qb), qb)
        dist = rel + (n * qb - k0)
        ok = (dist >= 0) & (dist <= DIL_WINDOW)
        keys = jnp.concatenate([kd_ref[0, krows, :], kd_ref[1, krows, :]], axis=0)
        s = lax.dot_general(qd_ref[qrows, :], keys, nt, preferred_element_type=F32)
        ms, ps = [], []
        for h in range(2):
            sh = jnp.where(ok, s[:, h * 2 * qb:(h + 1) * 2 * qb], NEG_INF)
            ms.append(jnp.max(sh, axis=-1, keepdims=True))
            ps.append(jnp.exp2(sh - ms[h]).astype(BF16))
        vals = jnp.concatenate([va_ref[0, krows, :], va_ref[1, krows, :]], axis=0)
        acc = jnp.dot(jnp.concatenate(ps, axis=1), vals, preferred_element_type=F32)
        den = acc[:, LANES:]
        dst = token_rows(n)
        o_ref[0, dst, :] = acc[:, :LANES] / den
        lse_ref[0, dst, :] = jnp.where(head[0], ms[0], ms[1]) + jnp.log2(den)

    unrolled(qblock)


def dilated_group(att3, group):
    bsz, seq, cols = att3.shape
    rate = DIL_RATES[group]
    cpb = seq // rate // DIL_QBLOCK
    assert cpb >= 2 and cpb & (cpb - 1) == 0 and (seq // DIL_QBLOCK) % DIL_UNROLL == 0
    hp = DIL_WIDTH // LANES
    base = COL_DIL // LANES
    qkv = lambda part: pl.BlockSpec(
        (1, seq, LANES), lambda b, p, part=part: (b, 0, base + (part * DIL_GROUPS + group) * hp + p))
    out = pl.BlockSpec((1, seq, LANES), lambda b, p: (b, 0, p))
    shp = jax.ShapeDtypeStruct((bsz, seq, DIL_WIDTH), F32)
    return pl.pallas_call(
        functools.partial(_dilated_kernel, seq=seq, rate=rate), out_shape=(shp, shp), grid=(bsz, hp),
        in_specs=[qkv(0), qkv(1), qkv(2)],
        out_specs=(out, out),
        scratch_shapes=[pltpu.VMEM((seq, LANES), BF16), pltpu.VMEM((2, seq, LANES), BF16),
                        pltpu.VMEM((2, seq, 2 * LANES), BF16)],
        compiler_params=_params("parallel", "parallel"), name=f"dilated_rate{rate}",
    )(att3, att3, att3)


def _ssd_kernel(*refs):
    xpx_ref, xpb_ref, st_ref = refs[-3:]

    @pl.when(pl.program_id(1) == 0)
    def _():
        xpx_ref[0:SSD_PAD, :] = jnp.zeros((SSD_PAD, SSM_INNER), F32)
        xpb_ref[0:SSD_PAD, :] = jnp.zeros((SSD_PAD, SSM_INNER), F32)
        st_ref[...] = jnp.zeros(st_ref.shape, F32)

    for c in range(SSD_STEP_CHUNKS):
        _ssd_chunk(slice(c * SSM_CHUNK, (c + 1) * SSM_CHUNK), *refs)


def _ssd_chunk(rows, z_ref, xs_ref, bc_ref, dt_ref, cwx_ref, cwb_ref, cbx_ref, cbb_ref, dtb_ref, alog_ref,
               dexp_ref, onorm_ref, tri_ref, triu_ref, exp_ref, o_ref, xpx_ref, xpb_ref, st_ref):
    L = SSM_CHUNK
    pad = SSD_PAD

    def conv_silu(src_ref, pad_ref, w_ref, b_ref):
        pad_ref[pad:, :] = src_ref[0, rows, :].astype(F32)
        acc = b_ref[...] + w_ref[0:1, :] * pad_ref[pl.ds(pad - SSM_CONV + 1, L), :]
        for k in range(1, SSM_CONV):
            acc = acc + w_ref[k:k + 1, :] * pad_ref[pl.ds(pad - SSM_CONV + 1 + k, L), :]
        pad_ref[0:pad, :] = pad_ref[L:L + pad, :]
        return _silu(acc)

    xs = conv_silu(xs_ref, xpx_ref, cwx_ref, cbx_ref)
    bc = conv_silu(bc_ref, xpb_ref, cwb_ref, cbb_ref)
    gn = SSM_GROUPS * SSM_STATE
    bm, cm = bc[:, :gn], bc[:, gn:]

    xr = dt_ref[0, rows, :] + dtb_ref[...]
    dt = jnp.maximum(xr, 0.0) + jnp.log(1.0 + jnp.exp(-jnp.abs(xr)))
    adt = dt * (-jnp.exp(alog_ref[...]) * math.log2(math.e))
    acs = _dot_exact_lhs(tri_ref[...], adt)
    acs_t = _dot_exact_rhs(adt.T, triu_ref[...])
    expand = exp_ref[...]
    dt_e = _dot_exact_rhs(dt, expand)
    acs_e = _dot_exact_rhs(acs, expand)
    xdt = xs * dt_e
    last = acs_e[L - 1:L, :]
    grow = jnp.exp2(acs_e)
    to_end = jnp.exp2(last - acs_e)
    chunk_decay = jnp.exp2(last)

    ll = lax.broadcasted_iota(jnp.int32, (L, L), 0)
    ss = lax.broadcasted_iota(jnp.int32, (L, L), 1)
    causal = ll >= ss
    gw = SSM_INNER // SSM_GROUPS
    hpg = SSM_HEADS // SSM_GROUPS
    lane = lax.broadcasted_iota(jnp.int32, (L, gw), 1)
    nt = (((1,), (1,)), ((), ()))
    tn = (((0,), (0,)), ((), ()))
    ys = []
    for g in range(SSM_GROUPS):
        cols = slice(g * gw, (g + 1) * gw)
        bg = bm[:, g * SSM_STATE:(g + 1) * SSM_STATE].astype(BF16)
        cg = cm[:, g * SSM_STATE:(g + 1) * SSM_STATE].astype(BF16)
        xg = xdt[:, cols]
        cb = lax.dot_general(cg, bg, nt, preferred_element_type=F32)
        st = st_ref[g]
        y = jnp.dot(cg, st.astype(BF16), preferred_element_type=F32) * grow[:, cols]
        new = lax.dot_general(bg, (xg * to_end[:, cols]).astype(BF16), tn, preferred_element_type=F32)
        st_ref[g] = chunk_decay[:, cols] * st + new
        for hh in range(hpg):
            h = g * hpg + hh
            diff = acs[:, h:h + 1] - acs_t[h:h + 1, :]
            mat = (cb * jnp.exp2(jnp.where(causal, diff, -jnp.inf))).astype(BF16)
            xh = jnp.where((lane >= hh * SSM_HEAD_DIM) & (lane < (hh + 1) * SSM_HEAD_DIM), xg, 0.0)
            y = y + jnp.dot(mat, xh.astype(BF16), preferred_element_type=F32)
        ys.append(y)
    y = jnp.concatenate(ys, axis=1) + xs * dexp_ref[...]
    yg = y * _silu(z_ref[0, rows, :].astype(F32))
    ms = jnp.mean(yg * yg, axis=-1, keepdims=True)
    o_ref[0, rows, :] = yg * lax.rsqrt(ms + NORM_EPS) * onorm_ref[...]


def ssd_mixer(mix3, dt3, conv_w, conv_b, dt_bias, a_log, d_exp, out_norm, consts):
    bsz, seq, _ = mix3.shape
    L = SSM_CHUNK
    w = SSM_INNER
    tri, triu, expand = consts
    step = SSD_STEP_CHUNKS * L if seq % (SSD_STEP_CHUNKS * L) == 0 else None
    assert step is not None, "sequence must hold a whole number of grid steps"
    col = lambda idx: pl.BlockSpec((1, step, w), lambda b, c, idx=idx: (b, c, idx))
    vecw = lambda rows, idx: pl.BlockSpec((rows, w), lambda b, c, idx=idx: (0, idx))
    vec = pl.BlockSpec((1, LANES), lambda b, c: (0, 0))
    sq = pl.BlockSpec((L, L), lambda b, c: (0, 0))
    return pl.pallas_call(
        _ssd_kernel, out_shape=jax.ShapeDtypeStruct((bsz, seq, w), F32), grid=(bsz, seq // step),
        in_specs=[col(MIX_Z // w), col(MIX_XBC // w), col(MIX_XBC // w + 1),
                  pl.BlockSpec((1, step, LANES), lambda b, c: (b, c, 0)),
                  vecw(SSM_CONV, 0), vecw(SSM_CONV, 1), vecw(1, 0), vecw(1, 1), vec, vec,
                  vecw(1, 0), vecw(1, 0), sq, sq, pl.BlockSpec((LANES, w), lambda b, c: (0, 0))],
        out_specs=pl.BlockSpec((1, step, w), lambda b, c: (b, c, 0)),
        scratch_shapes=[pltpu.VMEM((L + SSD_PAD, w), F32), pltpu.VMEM((L + SSD_PAD, w), F32),
                        pltpu.VMEM((SSM_GROUPS, SSM_STATE, w // SSM_GROUPS), F32)],
        compiler_params=_params("parallel", "arbitrary"), name="ssd_mixer",
    )(mix3, mix3, mix3, dt3, conv_w, conv_w, conv_b, conv_b, dt_bias, a_log, d_exp, out_norm,
      tri, triu, expand)


def _merge_kernel(x_ref, a_ref, m_ref, o0_ref, o1_ref, o2_ref, l0_ref, l1_ref, l2_ref, gl_ref, bg_ref,
                  wa_ref, wm_ref, wc_ref, wo_ref, out_ref):
    l0, l1, l2 = l0_ref[...], l1_ref[...], l2_ref[...]
    lmax = jnp.maximum(jnp.maximum(l0, l1), l2)
    e0, e1, e2 = jnp.exp2(l0 - lmax), jnp.exp2(l1 - lmax), jnp.exp2(l2 - lmax)
    cmix = (e0 * o0_ref[...] + e1 * o1_ref[...] + e2 * o2_ref[...]) / (e0 + e1 + e2)
    gates = _sigmoid(gl_ref[...] + bg_ref[...])
    d = D_MODEL
    mm = lambda v, w_ref: jnp.dot(v.astype(BF16), w_ref[...], preferred_element_type=F32)
    merged = (gates[:, :d] * mm(a_ref[...], wa_ref) + gates[:, d:2 * d] * mm(m_ref[...], wm_ref)
              + gates[:, 2 * d:] * mm(cmix, wc_ref))
    out_ref[...] = x_ref[...] + mm(merged, wo_ref)


def merge_branches(x2, a2, m2, dil, mix2, b_gate, wa, wm, wc, wo, tm=512):
    n, d = x2.shape
    tm = min(tm, n)
    row = lambda width, idx=0: pl.BlockSpec((tm, width), lambda i, idx=idx: (i, idx))
    full = lambda arr: pl.BlockSpec(arr.shape, lambda i: (0, 0), pipeline_mode=pl.Buffered(1))
    (o0, l0), (o1, l1), (o2, l2) = dil
    gw = N_BRANCH * d
    return pl.pallas_call(
        _merge_kernel, out_shape=jax.ShapeDtypeStruct((n, d), F32), grid=(n // tm,),
        in_specs=[row(d), row(MOBA_WIDTH), row(SSM_INNER)] + [row(DIL_WIDTH)] * 6
                 + [row(gw, MIX_GATE // gw), full(b_gate), full(wa), full(wm), full(wc), full(wo)],
        out_specs=row(d), compiler_params=_params("parallel"), name="merge_branches",
    )(x2, a2, m2, o0, o1, o2, l0, l1, l2, mix2, b_gate, wa, wm, wc, wo)


def _rms_bf16(x, gain):
    ms = jnp.mean(x * x, axis=-1, keepdims=True)
    return (x * lax.rsqrt(ms + NORM_EPS) * gain).astype(BF16)


def _ffn_ple_kernel(x_ref, xh_ref, p_ref, gf_ref, wup_ref, cw_ref, cb_ref, wd_ref, gp_ref, wg_ref, wp_ref,
                    o_ref, u_ref, buf_ref, h_ref, *, tm, tiles_per_seq):
    ck = FFN_CHUNK
    nck = FFN_DIM // ck
    u_ref[0:HALO, :] = _rms_bf16(xh_ref[...], gf_ref[...])
    u_ref[HALO:, :] = _rms_bf16(x_ref[...], gf_ref[...])
    keep = jnp.where(pl.program_id(0) % tiles_per_seq == 0, 0.0, 1.0)

    def up(c):
        u = u_ref[...]
        for half in range(2):
            buf = buf_ref.at[2 * (c % 2) + half]
            cols = slice(half * FFN_DIM + c * ck, half * FFN_DIM + (c + 1) * ck)
            val = jnp.dot(u, wup_ref[:, cols], preferred_element_type=F32)
            buf[0:HALO, :] = val[0:HALO, :] * keep
            buf[HALO:, :] = val[HALO:, :]

    def conv(c, half):
        buf = buf_ref.at[2 * (c % 2) + half]
        cols = slice(half * FFN_DIM + c * ck, half * FFN_DIM + (c + 1) * ck)
        acc = cb_ref[:, cols] + cw_ref[0:1, cols] * buf[pl.ds(HALO - FFN_CONV + 1, tm), :]
        for k in range(1, FFN_CONV):
            acc = acc + cw_ref[k:k + 1, cols] * buf[pl.ds(HALO - FFN_CONV + 1 + k, tm), :]
        return acc

    up(0)
    for c in range(nck):
        if c + 1 < nck:
            up(c + 1)
        h_ref[:, c * ck:(c + 1) * ck] = (_silu(conv(c, 0)) * conv(c, 1)).astype(BF16)
    acc = x_ref[...] + jnp.dot(h_ref[...], wd_ref[...], preferred_element_type=F32)
    pg = _sigmoid(jnp.dot(_rms_bf16(acc, gp_ref[...]), wg_ref[...], preferred_element_type=F32))
    o_ref[...] = acc + jnp.dot(p_ref[...].astype(BF16), wp_ref[...], preferred_element_type=F32) * pg


def ffn_ple(x2, p_all, layer, gain_ffn, w_up, conv_w, conv_b, w_down, gain_ple, w_gate, w_ple, seq, tm=512):
    n, d = x2.shape
    tm = min(tm, seq)
    hb = tm // HALO
    row = lambda width: pl.BlockSpec((tm, width), lambda i: (i, 0))
    p_rows = pl.BlockSpec((tm, PLE_DIM), lambda i: (layer * (n // tm) + i, 0))
    resident = lambda arr: pl.BlockSpec(arr.shape, lambda i: (0, 0), pipeline_mode=pl.Buffered(1))
    return pl.pallas_call(
        functools.partial(_ffn_ple_kernel, tm=tm, tiles_per_seq=seq // tm),
        out_shape=jax.ShapeDtypeStruct((n, d), F32), grid=(n // tm,),
        in_specs=[row(d), pl.BlockSpec((HALO, d), lambda i: (jnp.maximum(i * hb - 1, 0), 0)), p_rows,
                  resident(gain_ffn), resident(w_up), resident(conv_w), resident(conv_b), resident(w_down),
                  resident(gain_ple), resident(w_gate), resident(w_ple)],
        out_specs=row(d),
        scratch_shapes=[pltpu.VMEM((tm + HALO, d), BF16), pltpu.VMEM((4, tm + HALO, FFN_CHUNK), F32),
                        pltpu.VMEM((tm, FFN_DIM), BF16)],
        compiler_params=_params("parallel"), name="ffn_ple",
    )(x2, x2, p_all, gain_ffn, w_up, conv_w, conv_b, w_down, gain_ple, w_gate, w_ple)


def _constants():
    lane = np.arange(LANES)
    bd = (lane[:, None] // HEAD_DIM == lane[None, :] // HEAD_DIM).astype(np.float32) / HEAD_DIM
    r = np.arange(SSM_CHUNK)
    tri = (r[None, :] <= r[:, None]).astype(np.float32)
    expand = (lane[:, None] == (np.arange(SSM_INNER)[None, :] // SSM_HEAD_DIM)).astype(np.float32)
    as_bf16 = lambda a: jnp.asarray(a, dtype=BF16)
    return as_bf16(bd), (as_bf16(tri), as_bf16(tri.T), as_bf16(expand))


def _pad_lanes(v):
    return jnp.pad(v, (0, LANES - v.shape[0]))[None, :]


def kernel(x, p, positions, norm_mix, w_in, b_gate, moba_q_norm, moba_k_norm, dil_q_norm, dil_k_norm,
           ssm_conv_w, ssm_conv_b, ssm_dt_bias, ssm_a_log, ssm_d, ssm_out_norm, w_br_moba, w_br_ssm,
           w_br_dil, w_out, norm_ffn, w_up, ffn_conv_w, ffn_conv_b, w_down, norm_ple, w_ple_gate, w_ple):
    bsz, seq, d = x.shape
    depth = w_in.shape[0]
    n = bsz * seq
    bd, ssd_consts = _constants()
    tables = [t.reshape(n, LANES) for t in rope_tables(positions)]
    row = lambda v: v[None, :]
    heads = lambda v, count: jnp.tile(v, count)
    ones = lambda count: jnp.ones((count,), F32)

    x2 = x.reshape(n, d)
    for i in range(depth):
        w_all = w_in[i].astype(BF16)
        w_dt = jnp.pad(w_all[:, COL_DT:COL_GATE], ((0, 0), (0, LANES - SSM_HEADS)))
        half = ROPE_DIM // 2
        head_gain = jnp.stack([jnp.concatenate([
            heads(jnp.roll(moba_q_norm[i], s), MOBA_HEADS), heads(jnp.roll(moba_k_norm[i], s), MOBA_HEADS),
            ones(MOBA_WIDTH), heads(jnp.roll(dil_q_norm[i], s), DIL_GROUPS * DIL_HEADS),
            heads(jnp.roll(dil_k_norm[i], s), DIL_GROUPS * DIL_HEADS), ones(DIL_GROUPS * DIL_WIDTH)])
            for s in (0, half, -half)])
        att2, mix2, dt2 = in_projection(x2, row(norm_mix[i]), w_all, w_all[:, COL_GATE:], w_dt,
                                        head_gain, bd, tables)
        att3 = att2.reshape(bsz, seq, COL_Z)

        out_a = moba_attention(att3)
        out_b = ssd_mixer(mix2.reshape(bsz, seq, MIX_COLS), dt2.reshape(bsz, seq, LANES), ssm_conv_w[i],
                          row(ssm_conv_b[i]), _pad_lanes(ssm_dt_bias[i]), _pad_lanes(ssm_a_log[i]),
                          row(jnp.repeat(ssm_d[i], SSM_HEAD_DIM)), row(ssm_out_norm[i]), ssd_consts)
        dil = [dilated_group(att3, g) for g in range(DIL_GROUPS)]
        dil2 = [(o.reshape(n, DIL_WIDTH), l.reshape(n, DIL_WIDTH)) for o, l in dil]

        x2 = merge_branches(x2, out_a.reshape(n, MOBA_WIDTH), out_b.reshape(n, SSM_INNER), dil2, mix2,
                            row(b_gate[i]), w_br_moba[i].astype(BF16), w_br_ssm[i].astype(BF16),
                            w_br_dil[i].astype(BF16), w_out[i].astype(BF16))
        x2 = ffn_ple(x2, p.reshape(depth * n, PLE_DIM), i, row(norm_ffn[i]), w_up[i].astype(BF16), ffn_conv_w[i],
                     row(ffn_conv_b[i]), w_down[i].astype(BF16), row(norm_ple[i]),
                     w_ple_gate[i].astype(BF16), w_ple[i].astype(BF16), seq)
    return x2.reshape(bsz, seq, d)
```
